```python
import jax, jax.numpy as jnp
from jax import lax
import numpy as np

D_MODEL = 4096
BATCH = 4
SEQ = 2048
DEPTH = 4
DEC_BATCH = 8
DEC_SEQ = 64
PAST_LEN = 1024

CHUNK = 64
Q_BLOCK = 128
HEAD_DIM = 128
D_MIX = D_MODEL
H_SB = (3 * D_MIX // 8) // HEAD_DIM
D_SB = H_SB * HEAD_DIM
H_FOX = H_SB
D_FOX = H_FOX * HEAD_DIM
D_POOL = D_MIX - D_SB - D_FOX
POOL_WINDOWS = (2, 4, 8, 16)
N_POOL_GROUPS = 4
POOL_GROUP = D_POOL // N_POOL_GROUPS
POOL_BUF = 15
N_GROUPS = 4
EXPERTS_PER_GROUP = 4
N_EXPERTS = N_GROUPS * EXPERTS_PER_GROUP
TOP_K = 2
D_EXPERT = D_MODEL // 8
IN_SIZES = (D_SB, D_SB, D_SB, D_POOL, D_FOX, D_FOX, D_FOX)
N_IN = 3 * D_SB + D_POOL + 3 * D_FOX + H_FOX
FORGET_BIAS = 3.0
SCALE = HEAD_DIM ** -0.5
EPS = 1e-6

kernel_name = "hybrid_sb_pool_fox_hmoe_stream_step"


def rmsnorm(x, g):
    xf = x.astype(jnp.float32)
    y = xf * lax.rsqrt(jnp.mean(xf * xf, axis=-1, keepdims=True) + EPS)
    return (y * g.astype(jnp.float32)).astype(x.dtype)


def project_mixer_inputs(xn, w_in, b_forget, g_qn, g_kn):
    B, T, _ = xn.shape
    z = xn @ w_in
    cuts, acc = [], 0
    for s in IN_SIZES:
        acc += s
        cuts.append(acc)
    q_sb, k_sb, v_sb, u, q_fx, k_fx, v_fx, f = jnp.split(z, cuts, axis=-1)
    heads = lambda t, h: t.reshape(B, T, h, HEAD_DIM)
    q_fx = rmsnorm(heads(q_fx, H_FOX), g_qn)
    k_fx = rmsnorm(heads(k_fx, H_FOX), g_kn)
    log_f = jax.nn.log_sigmoid(f.astype(jnp.float32) + b_forget.astype(jnp.float32))
    return (heads(q_sb, H_SB), heads(k_sb, H_SB), heads(v_sb, H_SB), u,
            q_fx, k_fx, heads(v_fx, H_FOX), log_f)


def stick_breaking(q, k, v, q_pos, k_pos):
    z = jnp.einsum("bqhd,bkhd->bhqk", q, k, preferred_element_type=jnp.float32) * SCALE
    before = k_pos[None, :] < q_pos[:, None]
    log_keep = jnp.where(before, jax.nn.log_sigmoid(-z), 0.0)
    log_between = lax.cumsum(log_keep, axis=3, reverse=True) - log_keep
    a = jnp.where(before, jnp.exp(jax.nn.log_sigmoid(z) + log_between), 0.0)
    return jnp.einsum("bhqk,bkhd->bqhd", a.astype(v.dtype), v)


def forgetting_attention(q, k, v, cq, ck, q_pos, k_pos):
    s = jnp.einsum("bqhd,bkhd->bhqk", q, k, preferred_element_type=jnp.float32) * SCALE
    s = s + jnp.swapaxes(cq, 1, 2)[..., :, None] - jnp.swapaxes(ck, 1, 2)[..., None, :]
    s = jnp.where(k_pos[None, :] <= q_pos[:, None], s, -jnp.inf)
    p = jax.nn.softmax(s, axis=-1)
    return jnp.einsum("bhqk,bkhd->bqhd", p.astype(v.dtype), v)


def multiscale_pool(u_hist, u, pos, pool_w, pool_scale):
    B, T, _ = u.shape
    ext = jnp.concatenate([u_hist, u], axis=1).astype(jnp.float32)
    cs = jnp.concatenate([jnp.zeros_like(ext[:, :1]), jnp.cumsum(ext, axis=1)], axis=1)
    end = cs[:, POOL_BUF + 1:]
    uf = u.astype(jnp.float32)
    parts = []
    for g, w in enumerate(POOL_WINDOWS):
        sl = slice(g * POOL_GROUP, (g + 1) * POOL_GROUP)
        win_sum = end[..., sl] - cs[:, POOL_BUF + 1 - w:POOL_BUF + 1 - w + T, sl]
        count = jnp.minimum(w, pos + 1).astype(jnp.float32)
        parts.append(win_sum / count[None, :, None] - uf[..., sl])
    d = jnp.stack(parts, axis=2)
    y = jnp.einsum("btgc,gcd->btgd", d, pool_w.astype(jnp.float32)).reshape(B, T, D_POOL)
    return (y * pool_scale.astype(jnp.float32)).astype(u.dtype)


def hierarchical_moe(x, rg_w, rg_b, re_w, re_b, w_gate, w_up, w_down):
    N = x.shape[0]
    xf = x.astype(jnp.float32)
    g_logits = xf @ rg_w.astype(jnp.float32) + rg_b.astype(jnp.float32)
    g_prob = jax.nn.softmax(g_logits, axis=-1)
    g_sel = jnp.argmax(g_logits, axis=-1)
    g_w = jnp.take_along_axis(g_prob, g_sel[:, None], axis=-1)
    e_logits = (xf @ re_w.astype(jnp.float32) + re_b.astype(jnp.float32)).reshape(N, N_GROUPS, EXPERTS_PER_GROUP)
    e_in = jnp.take_along_axis(e_logits, g_sel[:, None, None], axis=1)[:, 0]
    top_l, top_i = lax.top_k(e_in, TOP_K)
    top_w = jax.nn.softmax(top_l, axis=-1) * g_w
    expert_id = g_sel[:, None] * EXPERTS_PER_GROUP + top_i
    gates = jnp.einsum("nk,nke->ne", top_w, jax.nn.one_hot(expert_id, N_EXPERTS, dtype=jnp.float32))
    out = jnp.zeros((N, x.shape[1]), jnp.float32)
    for g in range(N_GROUPS):
        es = slice(g * EXPERTS_PER_GROUP, (g + 1) * EXPERTS_PER_GROUP)
        h = jax.nn.silu(jnp.einsum("nd,edf->nef", x, w_gate[es])) * jnp.einsum("nd,edf->nef", x, w_up[es])
        h = h * gates[:, es, None].astype(h.dtype)
        out = out + jnp.einsum("nef,efd->nd", h, w_down[es], preferred_element_type=jnp.float32)
    return out.astype(x.dtype)


def finish_layer(x, o_sb, o_pool, o_fx, w_out, norm2_g, rg_w, rg_b, re_w, re_b, w_gate, w_up, w_down):
    B, T, D = x.shape
    mix = jnp.concatenate([o_sb.reshape(B, T, D_SB), o_pool, o_fx.reshape(B, T, D_FOX)], axis=-1)
    h = x + mix @ w_out
    f = hierarchical_moe(rmsnorm(h, norm2_g).reshape(B * T, D), rg_w, rg_b, re_w, re_b, w_gate, w_up, w_down)
    return h + f.reshape(B, T, D)


def to_blocks(a):
    B, T = a.shape[0], a.shape[1]
    return a.reshape((B, T // Q_BLOCK, Q_BLOCK) + a.shape[2:]).swapaxes(0, 1)


def from_blocks(a):
    nb, B, qb = a.shape[0], a.shape[1], a.shape[2]
    return a.swapaxes(0, 1).reshape((B, nb * qb) + a.shape[3:])


def setup_inputs(seed: int = 0) -> dict:
    key = jax.random.key(seed)
    ks = jax.random.split(key, 24)
    nrm = lambda k, shape, s: jax.random.normal(k, shape, jnp.float32) * s
    return {
        "x_prompt": nrm(ks[0], (BATCH, SEQ, D_MODEL), 1.0),
        "x_sample": nrm(ks[1], (DEC_BATCH, DEC_SEQ, D_MODEL), 1.0),
        "cache_sb_k": nrm(ks[2], (DEPTH, DEC_BATCH, PAST_LEN, H_SB, HEAD_DIM), 1.0),
        "cache_sb_v": nrm(ks[3], (DEPTH, DEC_BATCH, PAST_LEN, H_SB, HEAD_DIM), 1.0),
        "cache_fox_k": nrm(ks[4], (DEPTH, DEC_BATCH, PAST_LEN, H_FOX, HEAD_DIM), 1.0),
        "cache_fox_v": nrm(ks[5], (DEPTH, DEC_BATCH, PAST_LEN, H_FOX, HEAD_DIM), 1.0),
        "cache_fox_logf": jax.nn.log_sigmoid(FORGET_BIAS + nrm(ks[6], (DEPTH, DEC_BATCH, PAST_LEN, H_FOX), 1.0)),
        "state_pool": nrm(ks[7], (DEPTH, DEC_BATCH, POOL_BUF, D_POOL), 1.0),
        "norm1_g": 1.0 + nrm(ks[8], (DEPTH, D_MODEL), 0.02),
        "w_in": nrm(ks[9], (DEPTH, D_MODEL, N_IN), D_MODEL ** -0.5),
        "b_forget": FORGET_BIAS + nrm(ks[10], (DEPTH, H_FOX), 0.1),
        "fox_qnorm_g": 1.0 + nrm(ks[11], (DEPTH, HEAD_DIM), 0.02),
        "fox_knorm_g": 1.0 + nrm(ks[12], (DEPTH, HEAD_DIM), 0.02),
        "pool_w": nrm(ks[13], (DEPTH, N_POOL_GROUPS, POOL_GROUP, POOL_GROUP), POOL_GROUP ** -0.5),
        "pool_scale": 1.0 + nrm(ks[14], (DEPTH, D_POOL), 0.02),
        "w_out": nrm(ks[15], (DEPTH, D_MIX, D_MODEL), D_MIX ** -0.5),
        "norm2_g": 1.0 + nrm(ks[16], (DEPTH, D_MODEL), 0.02),
        "router_group_w": nrm(ks[17], (DEPTH, D_MODEL, N_GROUPS), D_MODEL ** -0.5),
        "router_group_b": nrm(ks[18], (DEPTH, N_GROUPS), 0.01),
        "router_expert_w": nrm(ks[19], (DEPTH, D_MODEL, N_EXPERTS), D_MODEL ** -0.5),
        "router_expert_b": nrm(ks[20], (DEPTH, N_EXPERTS), 0.01),
        "expert_w_gate": nrm(ks[21], (DEPTH, N_EXPERTS, D_MODEL, D_EXPERT), D_MODEL ** -0.5),
        "expert_w_up": nrm(ks[22], (DEPTH, N_EXPERTS, D_MODEL, D_EXPERT), D_MODEL ** -0.5),
        "expert_w_down": nrm(ks[23], (DEPTH, N_EXPERTS, D_EXPERT, D_MODEL), D_EXPERT ** -0.5),
    }


def reference(x_prompt, x_sample, cache_sb_k, cache_sb_v, cache_fox_k, cache_fox_v, cache_fox_logf, state_pool,
              norm1_g, w_in, b_forget, fox_qnorm_g, fox_knorm_g, pool_w, pool_scale, w_out, norm2_g,
              router_group_w, router_group_b, router_expert_w, router_expert_b,
              expert_w_gate, expert_w_up, expert_w_down):
    B, T = x_prompt.shape[0], x_prompt.shape[1]
    Ts = x_sample.shape[1]
    P = cache_sb_k.shape[2]
    pos_p = jnp.arange(T, dtype=jnp.int32)
    pos_blocks = pos_p.reshape(T // Q_BLOCK, Q_BLOCK)
    pos_s = P + jnp.arange(Ts, dtype=jnp.int32)
    kpos_s = jnp.arange(P + Ts, dtype=jnp.int32)
    xp, xs = x_prompt, x_sample
    sbk_p, sbv_p, fk_p, fv_p, fl_p, pl_p = [], [], [], [], [], []
    sbk_s, sbv_s, fk_s, fv_s, fl_s, pl_s = [], [], [], [], [], []
    for l in range(DEPTH):
        ffn = (norm2_g[l], router_group_w[l], router_group_b[l], router_expert_w[l], router_expert_b[l],
               expert_w_gate[l], expert_w_up[l], expert_w_down[l])
        q_sb, k_sb, v_sb, u, q_fx, k_fx, v_fx, log_f = project_mixer_inputs(
            rmsnorm(xp, norm1_g[l]), w_in[l], b_forget[l], fox_qnorm_g[l], fox_knorm_g[l])
        o_sb = from_blocks(lax.map(lambda a: stick_breaking(a[0], k_sb, v_sb, a[1], pos_p),
                                   (to_blocks(q_sb), pos_blocks)))
        c = jnp.cumsum(log_f, axis=1)
        o_fx = from_blocks(lax.map(lambda a: forgetting_attention(a[0], k_fx, v_fx, a[1], c, a[2], pos_p),
                                   (to_blocks(q_fx), to_blocks(c), pos_blocks)))
        o_pool = multiscale_pool(jnp.zeros((B, POOL_BUF, D_POOL), u.dtype), u, pos_p, pool_w[l], pool_scale[l])
        xp = finish_layer(xp, o_sb, o_pool, o_fx, w_out[l], *ffn)
        sbk_p.append(k_sb); sbv_p.append(v_sb); fk_p.append(k_fx); fv_p.append(v_fx)
        fl_p.append(log_f.astype(x_prompt.dtype)); pl_p.append(u[:, T - POOL_BUF:])
        q_sb, k_sb, v_sb, u, q_fx, k_fx, v_fx, log_f = project_mixer_inputs(
            rmsnorm(xs, norm1_g[l]), w_in[l], b_forget[l], fox_qnorm_g[l], fox_knorm_g[l])
        o_sb = stick_breaking(q_sb, jnp.concatenate([cache_sb_k[l], k_sb], axis=1),
                              jnp.concatenate([cache_sb_v[l], v_sb], axis=1), pos_s, kpos_s)
        c = jnp.cumsum(jnp.concatenate([cache_fox_logf[l].astype(jnp.float32), log_f], axis=1), axis=1)
        o_fx = forgetting_attention(q_fx, jnp.concatenate([cache_fox_k[l], k_fx], axis=1),
                                    jnp.concatenate([cache_fox_v[l], v_fx], axis=1), c[:, P:], c, pos_s, kpos_s)
        o_pool = multiscale_pool(state_pool[l], u, pos_s, pool_w[l], pool_scale[l])
        xs = finish_layer(xs, o_sb, o_pool, o_fx, w_out[l], *ffn)
        u_ext = jnp.concatenate([state_pool[l], u], axis=1)
        sbk_s.append(k_sb); sbv_s.append(v_sb); fk_s.append(k_fx); fv_s.append(v_fx)
        fl_s.append(log_f.astype(x_sample.dtype)); pl_s.append(u_ext[:, u_ext.shape[1] - POOL_BUF:])
    return (xp, xs,
            jnp.stack(sbk_p), jnp.stack(sbv_p), jnp.stack(fk_p), jnp.stack(fv_p), jnp.stack(fl_p), jnp.stack(pl_p),
            jnp.stack(sbk_s), jnp.stack(sbv_s), jnp.stack(fk_s), jnp.stack(fv_s), jnp.stack(fl_s), jnp.stack(pl_s))
```

```python
import functools

import jax
import jax.numpy as jnp
from jax import lax
from jax.experimental import pallas as pl
from jax.experimental.pallas import tpu as pltpu

F32 = jnp.float32
BF16 = jnp.bfloat16

D_MODEL = 4096
HEAD_DIM = 128
N_HEADS = 12
D_ATT = N_HEADS * HEAD_DIM
D_POOL = 1024
POOL_WINDOWS = (2, 4, 8, 16)
POOL_GROUP = 256
POOL_BUF = 15
POOL_HIST = 16
N_GROUPS = 4
EXPERTS_PER_GROUP = 4
N_EXPERTS = 16
N_PAIRS = 6
N_CLASSES = N_GROUPS * N_PAIRS
D_EXPERT = 512
N_IN = 3 * D_ATT + D_POOL + 3 * D_ATT + N_HEADS
LANE = 128
N_IN_PAD = ((N_IN + LANE - 1) // LANE) * LANE
SCALE = HEAD_DIM ** -0.5
EPS = 1e-6
NEG = -1e30

COL_QSB, COL_KSB, COL_VSB = 0, D_ATT, 2 * D_ATT
COL_U = 3 * D_ATT
COL_QFX = COL_U + D_POOL
COL_KFX = COL_QFX + D_ATT
COL_VFX = COL_KFX + D_ATT
COL_F = COL_VFX + D_ATT

VMEM_LIMIT = 56 * 1024 * 1024
MOE_TILE = 256


def _params(sem, vmem=VMEM_LIMIT):
    return pltpu.CompilerParams(dimension_semantics=sem, vmem_limit_bytes=vmem)


def _row_tile(n, target):
    best = 16
    for t in range(16, target + 1, 16):
        if n % t == 0:
            best = t
    assert n % best == 0
    return best


def _softplus(x):
    return jnp.maximum(x, 0.0) + jnp.log1p(jnp.exp(-jnp.abs(x)))


def _split_bf16(x):
    hi = x.astype(BF16)
    lo = (x - hi.astype(F32)).astype(BF16)
    return hi, lo


def _rmsnorm_kernel(x_ref, g_ref, o_ref):
    x = x_ref[...]
    ms = jnp.mean(x * x, axis=-1, keepdims=True)
    o_ref[...] = ((x * lax.rsqrt(ms + EPS)) * g_ref[...]).astype(o_ref.dtype)


def rmsnorm_bf16(x, g):
    n, d = x.shape
    tm = _row_tile(n, 256)
    return pl.pallas_call(
        _rmsnorm_kernel,
        out_shape=jax.ShapeDtypeStruct((n, d), BF16),
        grid=(n // tm,),
        in_specs=[pl.BlockSpec((tm, d), lambda i: (i, 0)), pl.BlockSpec((1, d), lambda i: (0, 0))],
        out_specs=pl.BlockSpec((tm, d), lambda i: (i, 0)),
        compiler_params=_params(("parallel",)),
        name="rmsnorm",
    )(x, g.reshape(1, d))


def _mm_kernel(a_ref, w_ref, o_ref):
    o_ref[...] = jnp.dot(a_ref[...], w_ref[...], preferred_element_type=F32)


def _mm_res_kernel(a_ref, w_ref, r_ref, o_ref):
    o_ref[...] = r_ref[...] + jnp.dot(a_ref[...], w_ref[...], preferred_element_type=F32)


def matmul(a, w, *, tm_target, tn, residual=None, name="matmul"):
    m, k = a.shape
    n = w.shape[1]
    tm = _row_tile(m, tm_target)
    assert n % tn == 0
    in_specs = [pl.BlockSpec((tm, k), lambda i, j: (i, 0)), pl.BlockSpec((k, tn), lambda i, j: (0, j))]
    args = [a, w]
    kern = _mm_kernel
    if residual is not None:
        in_specs.append(pl.BlockSpec((tm, tn), lambda i, j: (i, j)))
        args.append(residual)
        kern = _mm_res_kernel
    return pl.pallas_call(
        kern,
        out_shape=jax.ShapeDtypeStruct((m, n), F32),
        grid=(m // tm, n // tn),
        in_specs=in_specs,
        out_specs=pl.BlockSpec((tm, tn), lambda i, j: (i, j)),
        compiler_params=_params(("parallel", "arbitrary")),
        name=name,
    )(*args)


def _fox_prep_kernel(zq_ref, zk_ref, zf_ref, bf_ref, gq_ref, gk_ref, qn_ref, kn_ref, lf_ref):
    gq = gq_ref[...] * SCALE
    gk = gk_ref[...]
    for hh in range(zq_ref.shape[1] // HEAD_DIM):
        sl = slice(hh * HEAD_DIM, (hh + 1) * HEAD_DIM)
        q = zq_ref[:, sl]
        qn = q * lax.rsqrt(jnp.mean(q * q, axis=-1, keepdims=True) + EPS)
        qn_ref[:, sl] = (qn * gq).astype(qn_ref.dtype)
        k = zk_ref[:, sl]
        kn = k * lax.rsqrt(jnp.mean(k * k, axis=-1, keepdims=True) + EPS)
        kn_ref[:, sl] = kn * gk

    @pl.when(pl.program_id(1) == 0)
    def _():
        lf_ref[...] = -_softplus(-(zf_ref[...] + bf_ref[...]))


def fox_prep(z, b_forget_pad, gq, gk):
    n = z.shape[0]
    tm = _row_tile(n, 256)
    cw = 512
    return pl.pallas_call(
        _fox_prep_kernel,
        out_shape=(jax.ShapeDtypeStruct((n, D_ATT), BF16),
                   jax.ShapeDtypeStruct((n, D_ATT), F32),
                   jax.ShapeDtypeStruct((n, LANE), F32)),
        grid=(n // tm, D_ATT // cw),
        in_specs=[pl.BlockSpec((tm, cw), lambda i, j: (i, COL_QFX // cw + j)),
                  pl.BlockSpec((tm, cw), lambda i, j: (i, COL_KFX // cw + j)),
                  pl.BlockSpec((tm, LANE), lambda i, j: (i, COL_F // LANE)),
                  pl.BlockSpec((1, LANE), lambda i, j: (0, 0)),
                  pl.BlockSpec((1, HEAD_DIM), lambda i, j: (0, 0)),
                  pl.BlockSpec((1, HEAD_DIM), lambda i, j: (0, 0))],
        out_specs=(pl.BlockSpec((tm, cw), lambda i, j: (i, j)),
                   pl.BlockSpec((tm, cw), lambda i, j: (i, j)),
                   pl.BlockSpec((tm, LANE), lambda i, j: (i, 0))),
        compiler_params=_params(("parallel", "arbitrary")),
        name="fox_prep",
    )(z, z, z, b_forget_pad, gq.reshape(1, HEAD_DIM), gk.reshape(1, HEAD_DIM))


def _cumsum_kernel(x_ref, tri_ref, o_ref, *, chunk):
    tri = tri_ref[...]
    carry = jnp.zeros((1, x_ref.shape[1]), F32)
    for c in range(x_ref.shape[0] // chunk):
        x = x_ref[c * chunk:(c + 1) * chunk, :]
        hi = x.astype(BF16)
        r1 = x - hi.astype(F32)
        mid = r1.astype(BF16)
        lo = (r1 - mid.astype(F32)).astype(BF16)
        cs = (jnp.dot(tri, hi, preferred_element_type=F32)
              + jnp.dot(tri, mid, preferred_element_type=F32)
              + jnp.dot(tri, lo, preferred_element_type=F32)) + carry
        o_ref[c * chunk:(c + 1) * chunk, :] = cs
        carry = cs[chunk - 1:chunk, :]


def cumsum_time(x):
    b, length, w = x.shape
    chunk = next(c for c in (256, 128, 64, 32, 16, 8) if length % c == 0)
    tri = jnp.tril(jnp.ones((chunk, chunk), BF16))
    return pl.pallas_call(
        functools.partial(_cumsum_kernel, chunk=chunk),
        out_shape=jax.ShapeDtypeStruct((b, length, w), F32),
        grid=(b,),
        in_specs=[pl.BlockSpec((None, length, w), lambda i: (i, 0, 0)),
                  pl.BlockSpec((chunk, chunk), lambda i: (0, 0))],
        out_specs=pl.BlockSpec((None, length, w), lambda i: (i, 0, 0)),
        compiler_params=_params(("parallel",)),
        name="cumsum_time",
    )(x, tri)


def _sb_block(q, k, v, tri, r_in, before):
    zz = lax.dot_general(q, k, (((1,), (1,)), ((), ())), preferred_element_type=F32)
    lk = -_softplus(zz)
    if before is not None:
        lk = jnp.where(before, lk, 0.0)
    hi, lo = _split_bf16(lk)
    rc = jnp.dot(hi, tri, preferred_element_type=F32) + jnp.dot(lo, tri, preferred_element_type=F32)
    e = jnp.exp(zz + rc + r_in)
    if before is not None:
        e = jnp.where(before, e, 0.0)
    pv = jnp.dot(e.astype(BF16), v, preferred_element_type=F32)
    return pv, r_in + rc[:, 0:1]


def _sb_prompt_kernel(q_ref, k_ref, v_ref, tri_ref, o_ref, kb, vb, *, tq, nk):
    qi = pl.program_id(2)

    @pl.when(qi == 0)
    def _():
        for c in range(nk):
            kb[c] = k_ref[c * tq:(c + 1) * tq, :].astype(BF16)
            vb[c] = v_ref[c * tq:(c + 1) * tq, :].astype(BF16)

    q = (q_ref[...] * SCALE).astype(BF16)
    tri = tri_ref[...]
    row = lax.broadcasted_iota(jnp.int32, (tq, tq), 0)
    col = lax.broadcasted_iota(jnp.int32, (tq, tq), 1)
    acc0, r0 = _sb_block(q, kb[qi], vb[qi], tri, jnp.zeros((tq, 1), F32), col < row)

    def body(jj, carry):
        acc, r = carry
        j = qi - 1 - jj
        pv, r2 = _sb_block(q, kb[j], vb[j], tri, r, None)
        return acc + pv, r2

    acc, _ = lax.fori_loop(0, qi, body, (acc0, r0))
    o_ref[...] = acc.astype(o_ref.dtype)


def sb_attention_prompt(z, n_batch, seq, tq=256):
    nq = seq // tq
    tri = jnp.tril(jnp.ones((tq, tq), BF16))
    hb = HEAD_DIM
    return pl.pallas_call(
        functools.partial(_sb_prompt_kernel, tq=tq, nk=nq),
        out_shape=jax.ShapeDtypeStruct((n_batch * seq, D_ATT), BF16),
        grid=(n_batch, N_HEADS, nq),
        in_specs=[pl.BlockSpec((tq, hb), lambda b, h, i: (b * nq + i, COL_QSB // hb + h)),
                  pl.BlockSpec((seq, hb), lambda b, h, i: (b, COL_KSB // hb + h)),
                  pl.BlockSpec((seq, hb), lambda b, h, i: (b, COL_VSB // hb + h)),
                  pl.BlockSpec((tq, tq), lambda b, h, i: (0, 0))],
        out_specs=pl.BlockSpec((tq, hb), lambda b, h, i: (b * nq + i, h)),
        scratch_shapes=[pltpu.VMEM((nq, tq, hb), BF16), pltpu.VMEM((nq, tq, hb), BF16)],
        compiler_params=_params(("parallel", "parallel", "arbitrary")),
        name="sb_prompt",
    )(z, z, z, tri)


def _sb_sample_kernel(q_ref, kn_ref, vn_ref, kc_ref, vc_ref, tri_n_ref, tri_c_ref, o_ref, *, ts, past, ck):
    q = (q_ref[...] * SCALE).astype(BF16)
    row = lax.broadcasted_iota(jnp.int32, (ts, ts), 0)
    col = lax.broadcasted_iota(jnp.int32, (ts, ts), 1)
    acc, r = _sb_block(q, kn_ref[...].astype(BF16), vn_ref[...].astype(BF16), tri_n_ref[...],
                       jnp.zeros((ts, 1), F32), col < row)
    tri_c = tri_c_ref[...]
    for c in range(past // ck - 1, -1, -1):
        kc = kc_ref[c * ck:(c + 1) * ck, :].astype(BF16)
        vc = vc_ref[c * ck:(c + 1) * ck, :].astype(BF16)
        pv, r = _sb_block(q, kc, vc, tri_c, r, None)
        acc = acc + pv
    o_ref[...] = acc.astype(o_ref.dtype)


def sb_attention_sample(z, cache_k, cache_v, row0, n_batch, ts, past):
    hb = HEAD_DIM
    ck = next(c for c in (256, 128, 64, 32, 16, 8) if past % c == 0)
    tri_n = jnp.tril(jnp.ones((ts, ts), BF16))
    tri_c = jnp.tril(jnp.ones((ck, ck), BF16))
    rb = row0 // ts
    return pl.pallas_call(
        functools.partial(_sb_sample_kernel, ts=ts, past=past, ck=ck),
        out_shape=jax.ShapeDtypeStruct((n_batch * ts, D_ATT), BF16),
        grid=(n_batch, N_HEADS),
        in_specs=[pl.BlockSpec((ts, hb), lambda b, h: (rb + b, COL_QSB // hb + h)),
                  pl.BlockSpec((ts, hb), lambda b, h: (rb + b, COL_KSB // hb + h)),
                  pl.BlockSpec((ts, hb), lambda b, h: (rb + b, COL_VSB // hb + h)),
                  pl.BlockSpec((past, hb), lambda b, h: (b, h)),
                  pl.BlockSpec((past, hb), lambda b, h: (b, h)),
                  pl.BlockSpec((ts, ts), lambda b, h: (0, 0)),
                  pl.BlockSpec((ck, ck), lambda b, h: (0, 0))],
        out_specs=pl.BlockSpec((ts, hb), lambda b, h: (b, h)),
        compiler_params=_params(("parallel", "parallel")),
        name="sb_sample",
    )(z, z, z, cache_k, cache_v, tri_n, tri_c)


def _fox_prompt_kernel(q_ref, k_ref, v_ref, cq_ref, ck_ref, o_ref, kb, vb, *, tq, nk):
    qi = pl.program_id(2)

    @pl.when(qi == 0)
    def _():
        for c in range(nk):
            kb[c] = k_ref[c * tq:(c + 1) * tq, :].astype(BF16)
            vb[c] = v_ref[c * tq:(c + 1) * tq, :].astype(BF16)

    q = q_ref[...]
    cq = cq_ref[...]

    def scores(j):
        s = lax.dot_general(q, kb[j], (((1,), (1,)), ((), ())), preferred_element_type=F32)
        return s + cq - ck_ref[j]

    row = lax.broadcasted_iota(jnp.int32, (tq, tq), 0)
    col = lax.broadcasted_iota(jnp.int32, (tq, tq), 1)
    s = jnp.where(col <= row, scores(qi), NEG)
    m0 = jnp.max(s, axis=-1, keepdims=True)
    p = jnp.exp(s - m0)
    l0 = jnp.sum(p, axis=-1, keepdims=True)
    acc0 = jnp.dot(p.astype(BF16), vb[qi], preferred_element_type=F32)

    def body(j, carry):
        m, l, acc = carry
        s = scores(j)
        m_new = jnp.maximum(m, jnp.max(s, axis=-1, keepdims=True))
        alpha = jnp.exp(m - m_new)
        p = jnp.exp(s - m_new)
        l = alpha * l + jnp.sum(p, axis=-1, keepdims=True)
        acc = alpha * acc + jnp.dot(p.astype(BF16), vb[j], preferred_element_type=F32)
        return m_new, l, acc

    _, l, acc = lax.fori_loop(0, qi, body, (m0, l0, acc0))
    o_ref[...] = (acc / l).astype(o_ref.dtype)


def fox_attention_prompt(qn, kn, z, cq, ck, n_batch, seq, tq=256):
    nq = seq // tq
    hb = HEAD_DIM
    return pl.pallas_call(
        functools.partial(_fox_prompt_kernel, tq=tq, nk=nq),
        out_shape=jax.ShapeDtypeStruct((n_batch * seq, D_ATT), BF16),
        grid=(n_batch, N_HEADS, nq),
        in_specs=[pl.BlockSpec((tq, hb), lambda b, h, i: (b * nq + i, h)),
                  pl.BlockSpec((seq, hb), lambda b, h, i: (b, h)),
                  pl.BlockSpec((seq, hb), lambda b, h, i: (b, COL_VFX // hb + h)),
                  pl.BlockSpec((None, tq, 1), lambda b, h, i: (h, b * nq + i, 0)),
                  pl.BlockSpec((None, nq, 1, tq), lambda b, h, i: (b * N_HEADS + h, 0, 0, 0))],
        out_specs=pl.BlockSpec((tq, hb), lambda b, h, i: (b * nq + i, h)),
        scratch_shapes=[pltpu.VMEM((nq, tq, hb), BF16), pltpu.VMEM((nq, tq, hb), BF16)],
        compiler_params=_params(("parallel", "parallel", "arbitrary")),
        name="fox_prompt",
    )(qn, kn, z, cq, ck)


def _fox_sample_kernel(q_ref, kn_ref, vn_ref, kc_ref, vc_ref, cq_ref, ckn_ref, ckc_ref, o_ref, *, ts):
    q = q_ref[...]
    cq = cq_ref[...]
    dn = (((1,), (1,)), ((), ()))
    s_c = lax.dot_general(q, kc_ref[...].astype(BF16), dn, preferred_element_type=F32) + cq - ckc_ref[...]
    s_n = lax.dot_general(q, kn_ref[...].astype(BF16), dn, preferred_element_type=F32) + cq - ckn_ref[...]
    row = lax.broadcasted_iota(jnp.int32, (ts, ts), 0)
    col = lax.broadcasted_iota(jnp.int32, (ts, ts), 1)
    s_n = jnp.where(col <= row, s_n, NEG)
    m = jnp.maximum(jnp.max(s_c, axis=-1, keepdims=True), jnp.max(s_n, axis=-1, keepdims=True))
    p_c = jnp.exp(s_c - m)
    p_n = jnp.exp(s_n - m)
    l = jnp.sum(p_c, axis=-1, keepdims=True) + jnp.sum(p_n, axis=-1, keepdims=True)
    acc = (jnp.dot(p_c.astype(BF16), vc_ref[...].astype(BF16), preferred_element_type=F32)
           + jnp.dot(p_n.astype(BF16), vn_ref[...].astype(BF16), preferred_element_type=F32))
    o_ref[...] = (acc / l).astype(o_ref.dtype)


def fox_attention_sample(qn, kn, z, cache_k, cache_v, cq, ckn, ckc, row0, n_batch, ts, past):
    hb = HEAD_DIM
    rb = row0 // ts
    return pl.pallas_call(
        functools.partial(_fox_sample_kernel, ts=ts),
        out_shape=jax.ShapeDtypeStruct((n_batch * ts, D_ATT), BF16),
        grid=(n_batch, N_HEADS),
        in_specs=[pl.BlockSpec((ts, hb), lambda b, h: (rb + b, h)),
                  pl.BlockSpec((ts, hb), lambda b, h: (rb + b, h)),
                  pl.BlockSpec((ts, hb), lambda b, h: (rb + b, COL_VFX // hb + h)),
                  pl.BlockSpec((past, hb), lambda b, h: (b, h)),
                  pl.BlockSpec((past, hb), lambda b, h: (b, h)),
                  pl.BlockSpec((None, ts, 1), lambda b, h: (b * N_HEADS + h, 0, 0)),
                  pl.BlockSpec((None, 1, ts), lambda b, h: (b * N_HEADS + h, 0, 0)),
                  pl.BlockSpec((None, 1, past), lambda b, h: (b * N_HEADS + h, 0, 0))],
        out_specs=pl.BlockSpec((ts, hb), lambda b, h: (b, h)),
        compiler_params=_params(("parallel", "parallel")),
        name="fox_sample",
    )(qn, kn, z, cache_k, cache_v, cq, ckn, ckc)


def _pool_kernel(u_ref, hist_ref, w_ref, sc_ref, o_ref, ext, *, seq, pos0, chunk):
    g = pl.program_id(1)
    ext[0:POOL_HIST, :] = hist_ref[...]
    ext[POOL_HIST:POOL_HIST + seq, :] = u_ref[...]
    w = w_ref[...].astype(BF16)
    sc = sc_ref[...]
    for gi, win in enumerate(POOL_WINDOWS):
        @pl.when(g == gi)
        def _(win=win):
            for c in range(seq // chunk):
                base = POOL_HIST + c * chunk
                ws = ext[base:base + chunk, :]
                for i in range(1, win):
                    ws = ws + ext[base - i:base - i + chunk, :]
                t = lax.broadcasted_iota(jnp.int32, (chunk, 1), 0) + (pos0 + c * chunk + 1)
                cnt = jnp.minimum(win, t).astype(F32)
                d = ws / cnt - ext[base:base + chunk, :]
                y = jnp.dot(d.astype(BF16), w, preferred_element_type=F32) * sc
                o_ref[c * chunk:(c + 1) * chunk, :] = y.astype(o_ref.dtype)


def pool_mixer(z, hist, pool_w, pool_scale, row0, n_batch, seq, pos0):
    pg = POOL_GROUP
    chunk = min(seq, 256)
    rb = row0 // seq
    return pl.pallas_call(
        functools.partial(_pool_kernel, seq=seq, pos0=pos0, chunk=chunk),
        out_shape=jax.ShapeDtypeStruct((n_batch * seq, D_POOL), BF16),
        grid=(n_batch, len(POOL_WINDOWS)),
        in_specs=[pl.BlockSpec((seq, pg), lambda b, g: (rb + b, COL_U // pg + g)),
                  pl.BlockSpec((None, POOL_HIST, pg), lambda b, g: (b, 0, g)),
                  pl.BlockSpec((None, pg, pg), lambda b, g: (g, 0, 0)),
                  pl.BlockSpec((1, pg), lambda b, g: (0, g))],
        out_specs=pl.BlockSpec((seq, pg), lambda b, g: (b, g)),
        scratch_shapes=[pltpu.VMEM((POOL_HIST + seq, pg), F32)],
        compiler_params=_params(("parallel", "arbitrary")),
        name="pool_mixer",
    )(z, hist, pool_w, pool_scale.reshape(1, D_POOL))


def _router_kernel(h_ref, g_ref, w_ref, b_ref, meta_ref):
    h = h_ref[...]
    xn = (h * lax.rsqrt(jnp.mean(h * h, axis=-1, keepdims=True) + EPS)) * g_ref[...]
    w = w_ref[...]
    xh, xl = _split_bf16(xn)
    wh, wl = _split_bf16(w)
    lg = (jnp.dot(xh, wh, preferred_element_type=F32)
          + (jnp.dot(xh, wl, preferred_element_type=F32) + jnp.dot(xl, wh, preferred_element_type=F32))
          ) + b_ref[...]
    tm = lg.shape[0]
    lane = lax.broadcasted_iota(jnp.int32, (tm, LANE), 1)
    big = jnp.int32(LANE)
    gl = jnp.where(lane < N_GROUPS, lg, NEG)
    gmax = jnp.max(gl, axis=-1, keepdims=True)
    gsel = jnp.min(jnp.where(gl == gmax, lane, big), axis=-1, keepdims=True)
    gsum = jnp.sum(jnp.where(lane < N_GROUPS, jnp.exp(gl - gmax), 0.0), axis=-1, keepdims=True)
    gw = 1.0 / gsum
    lo_lane = N_GROUPS + EXPERTS_PER_GROUP * gsel
    el = jnp.where((lane >= lo_lane) & (lane < lo_lane + EXPERTS_PER_GROUP), lg, NEG)
    t1 = jnp.max(el, axis=-1, keepdims=True)
    i1 = jnp.min(jnp.where(el == t1, lane, big), axis=-1, keepdims=True)
    el2 = jnp.where(lane == i1, NEG, el)
    t2 = jnp.max(el2, axis=-1, keepdims=True)
    i2 = jnp.min(jnp.where(el2 == t2, lane, big), axis=-1, keepdims=True)
    e2 = jnp.exp(t2 - t1)
    den = 1.0 + e2
    w1 = gw * (1.0 / den)
    w2 = gw * (e2 / den)
    a1 = i1 - lo_lane
    a2 = i2 - lo_lane
    first_low = a1 < a2
    e_lo = jnp.where(first_low, a1, a2)
    e_hi = jnp.where(first_low, a2, a1)
    g_lo = jnp.where(first_low, w1, w2)
    g_hi = jnp.where(first_low, w2, w1)
    pair = (e_lo * (7 - e_lo)) // 2 + (e_hi - e_lo - 1)
    cls = (gsel * N_PAIRS + pair).astype(F32)
    meta_ref[...] = jnp.where(lane == 0, g_lo, jnp.where(lane == 1, g_hi, jnp.where(lane == 2, cls, 0.0)))


def router(h, g2, w_router, b_router):
    n, d = h.shape
    tm = _row_tile(n, 256)
    return pl.pallas_call(
        _router_kernel,
        out_shape=jax.ShapeDtypeStruct((n, LANE), F32),
        grid=(n // tm,),
        in_specs=[pl.BlockSpec((tm, d), lambda i: (i, 0)),
                  pl.BlockSpec((1, d), lambda i: (0, 0)),
                  pl.BlockSpec((d, LANE), lambda i: (0, 0)),
                  pl.BlockSpec((1, LANE), lambda i: (0, 0))],
        out_specs=pl.BlockSpec((tm, LANE), lambda i: (i, 0)),
        compiler_params=_params(("parallel",)),
        name="router",
    )(h, g2.reshape(1, d), w_router, b_router)


def _row_copy(src_hbm, dst_vmem, sem, src_row, dst_row):
    return pltpu.make_async_copy(src_hbm.at[pl.ds(src_row, 1), :], dst_vmem.at[pl.ds(dst_row, 1), :], sem)


def _gather_rows(idx_ref, base, src_hbm, dst_vmem, sem, n_rows):
    def issue(r, c):
        _row_copy(src_hbm, dst_vmem, sem, idx_ref[base + r], r).start()
        return c

    lax.fori_loop(0, n_rows, issue, 0)
    pltpu.make_async_copy(src_hbm.at[pl.ds(0, n_rows), :], dst_vmem, sem).wait()


def _gather_kernel(idx_ref, src_hbm, o_ref, sem, *, tm):
    _gather_rows(idx_ref, pl.program_id(0) * tm, src_hbm, o_ref, sem, tm)


def gather_rows(src, idx, tm):
    n_out = idx.shape[0]
    d = src.shape[1]
    return pl.pallas_call(
        functools.partial(_gather_kernel, tm=tm),
        out_shape=jax.ShapeDtypeStruct((n_out, d), src.dtype),
        grid_spec=pltpu.PrefetchScalarGridSpec(
            num_scalar_prefetch=1,
            grid=(n_out // tm,),
            in_specs=[pl.BlockSpec(memory_space=pl.ANY)],
            out_specs=pl.BlockSpec((tm, d), lambda t, idx: (t, 0)),
            scratch_shapes=[pltpu.SemaphoreType.DMA(())]),
        compiler_params=_params(("arbitrary",)),
        name="gather_rows",
    )(idx, src)


def _combine_kernel(idx_ref, h_ref, f_hbm, o_ref, buf, sem, *, tm):
    _gather_rows(idx_ref, pl.program_id(0) * tm, f_hbm, buf, sem, tm)
    o_ref[...] = h_ref[...] + buf[...]


def combine_rows(h, f_sorted, dest, tm):
    n, d = h.shape
    return pl.pallas_call(
        functools.partial(_combine_kernel, tm=tm),
        out_shape=jax.ShapeDtypeStruct((n, d), F32),
        grid_spec=pltpu.PrefetchScalarGridSpec(
            num_scalar_prefetch=1,
            grid=(n // tm,),
            in_specs=[pl.BlockSpec((tm, d), lambda t, idx: (t, 0)), pl.BlockSpec(memory_space=pl.ANY)],
            out_specs=pl.BlockSpec((tm, d), lambda t, idx: (t, 0)),
            scratch_shapes=[pltpu.VMEM((tm, d), F32), pltpu.SemaphoreType.DMA(())]),
        compiler_params=_params(("arbitrary",)),
        name="combine_rows",
    )(dest, h, f_sorted)


def _moe_kernel(texp_ref, tsel_ref, tvalid_ref, hs_ref, gate_ref, g2_ref, wg_ref, wu_ref, wd_ref, o_ref, xn):
    t = pl.program_id(0)
    k = pl.program_id(1)
    valid = tvalid_ref[t] == 1

    @pl.when(jnp.logical_and(valid, k == 0))
    def _():
        h = hs_ref[...]
        y = (h * lax.rsqrt(jnp.mean(h * h, axis=-1, keepdims=True) + EPS)) * g2_ref[...]
        xn[...] = y.astype(xn.dtype)

    @pl.when(valid)
    def _():
        x = xn[...]
        g = jnp.dot(x, wg_ref[...], preferred_element_type=F32)
        u = jnp.dot(x, wu_ref[...], preferred_element_type=F32)
        gates = gate_ref[...]
        gate = jnp.where(tsel_ref[2 * t + k] == 0, gates[:, 0:1], gates[:, 1:2])
        hm = ((g * (1.0 / (1.0 + jnp.exp(-g)))) * u) * gate
        contrib = jnp.dot(hm.astype(BF16), wd_ref[...], preferred_element_type=F32)

        @pl.when(k == 0)
        def _():
            o_ref[...] = contrib

        @pl.when(k == 1)
        def _():
            o_ref[...] = o_ref[...] + contrib

    @pl.when(jnp.logical_and(jnp.logical_not(valid), k == 0))
    def _():
        o_ref[...] = jnp.zeros_like(o_ref)


def moe_sorted(hs, gates_sorted, g2, wg, wu, wd, texp, tsel, tvalid, tm):
    npad, d = hs.shape
    n_tiles = npad // tm
    return pl.pallas_call(
        _moe_kernel,
        out_shape=jax.ShapeDtypeStruct((npad, d), F32),
        grid_spec=pltpu.PrefetchScalarGridSpec(
            num_scalar_prefetch=3,
            grid=(n_tiles, 2),
            in_specs=[pl.BlockSpec((tm, d), lambda t, k, te, ts, tv: (t, 0)),
                      pl.BlockSpec((tm, LANE), lambda t, k, te, ts, tv: (t, 0)),
                      pl.BlockSpec((1, d), lambda t, k, te, ts, tv: (0, 0)),
                      pl.BlockSpec((None, d, D_EXPERT), lambda t, k, te, ts, tv: (te[2 * t + k], 0, 0)),
                      pl.BlockSpec((None, d, D_EXPERT), lambda t, k, te, ts, tv: (te[2 * t + k], 0, 0)),
                      pl.BlockSpec((None, D_EXPERT, d), lambda t, k, te, ts, tv: (te[2 * t + k], 0, 0))],
            out_specs=pl.BlockSpec((tm, d), lambda t, k, te, ts, tv: (t, 0)),
            scratch_shapes=[pltpu.VMEM((tm, d), BF16)]),
        compiler_params=_params(("arbitrary", "arbitrary")),
        name="moe_sorted",
    )(texp, tsel, tvalid, hs, gates_sorted, g2.reshape(1, d), wg, wu, wd)


_PAIR_LO = (0, 0, 0, 1, 1, 2)
_PAIR_HI = (1, 2, 3, 2, 3, 3)


def _route_tables(cls, tm):
    n = cls.shape[0]
    n_tiles = n // tm + N_CLASSES
    onehot = (cls[:, None] == jnp.arange(N_CLASSES, dtype=jnp.int32)[None, :]).astype(jnp.int32)
    csum = jnp.cumsum(onehot, axis=0)
    counts = csum[-1]
    rank = jnp.take_along_axis(csum, cls[:, None], axis=1)[:, 0] - 1
    padded = ((counts + tm - 1) // tm) * tm
    ends = jnp.cumsum(padded)
    starts = ends - padded
    dest = (starts[cls] + rank).astype(jnp.int32)
    src = jnp.zeros((n_tiles * tm,), jnp.int32).at[dest].set(jnp.arange(n, dtype=jnp.int32))
    n_used = ends[-1] // tm
    tile = jnp.arange(n_tiles, dtype=jnp.int32)
    tvalid = (tile < n_used).astype(jnp.int32)
    tile_c = jnp.minimum(tile, n_used - 1)
    tcls = jnp.searchsorted(ends, tile_c * tm, side="right").astype(jnp.int32)
    group = tcls // N_PAIRS
    pair = tcls % N_PAIRS
    e_lo = group * EXPERTS_PER_GROUP + jnp.asarray(_PAIR_LO, jnp.int32)[pair]
    e_hi = group * EXPERTS_PER_GROUP + jnp.asarray(_PAIR_HI, jnp.int32)[pair]
    flip = ((tile_c * tm - starts[tcls]) // tm) % 2
    first = jnp.where(flip == 1, e_hi, e_lo)
    second = jnp.where(flip == 1, e_lo, e_hi)
    texp = jnp.stack([first, second], axis=1).reshape(-1).astype(jnp.int32)
    tsel = jnp.stack([flip, 1 - flip], axis=1).reshape(-1).astype(jnp.int32)
    return dest, src, texp, tsel, tvalid


def _layer(x, cache, w, dims):
    n_p, n_s, bp, tp, bs, ts, past = dims
    n = n_p + n_s
    (c_sbk, c_sbv, c_fxk, c_fxv, c_logf, c_pool) = cache

    xn = rmsnorm_bf16(x, w["norm1_g"])
    z = matmul(xn, w["w_in"], tm_target=544, tn=1152, name="in_proj")

    qn, kn, logf = fox_prep(z, w["b_forget"], w["fox_qnorm_g"], w["fox_knorm_g"])

    c_p = cumsum_time(logf[:n_p].reshape(bp, tp, LANE))[:, :, :N_HEADS]
    lf_s = jnp.concatenate([jnp.pad(c_logf, ((0, 0), (0, 0), (0, LANE - N_HEADS))),
                            logf[n_p:].reshape(bs, ts, LANE)], axis=1)
    c_s = cumsum_time(lf_s)[:, :, :N_HEADS]

    tq = min(256, tp)
    cq_p = jnp.transpose(c_p.reshape(n_p, N_HEADS))[:, :, None]
    ck_p = jnp.transpose(c_p, (0, 2, 1)).reshape(bp * N_HEADS, tp // tq, 1, tq)
    c_s_t = jnp.transpose(c_s, (0, 2, 1)).reshape(bs * N_HEADS, past + ts)
    cq_s = c_s_t[:, past:, None]
    ckn_s = c_s_t[:, None, past:]
    ckc_s = c_s_t[:, None, :past]

    o_sb_p = sb_attention_prompt(z, bp, tp, tq)
    o_fx_p = fox_attention_prompt(qn, kn, z, cq_p, ck_p, bp, tp, tq)
    o_pl_p = pool_mixer(z, jnp.zeros((bp, POOL_HIST, D_POOL), F32), w["pool_w"], w["pool_scale"], 0, bp, tp, 0)

    o_sb_s = sb_attention_sample(z, c_sbk, c_sbv, n_p, bs, ts, past)
    o_fx_s = fox_attention_sample(qn, kn, z, c_fxk, c_fxv, cq_s, ckn_s, ckc_s, n_p, bs, ts, past)
    hist_s = jnp.pad(c_pool, ((0, 0), (POOL_HIST - POOL_BUF, 0), (0, 0)))
    o_pl_s = pool_mixer(z, hist_s, w["pool_w"], w["pool_scale"], n_p, bs, ts, past)

    mix = jnp.concatenate([jnp.concatenate([o_sb_p, o_pl_p, o_fx_p], axis=1),
                           jnp.concatenate([o_sb_s, o_pl_s, o_fx_s], axis=1)], axis=0)
    h = matmul(mix, w["w_out"], tm_target=544, tn=1024, residual=x, name="out_proj")

    meta = router(h, w["norm2_g"], w["w_router"], w["b_router"])
    cls = meta[:, 2].astype(jnp.int32)
    dest, src, texp, tsel, tvalid = _route_tables(cls, MOE_TILE)
    hs = gather_rows(h, src, MOE_TILE)
    gates_sorted = jnp.take(meta, src, axis=0)
    fs = moe_sorted(hs, gates_sorted, w["norm2_g"], w["w_gate"], w["w_up"], w["w_down"],
                    texp, tsel, tvalid, MOE_TILE)
    y = combine_rows(h, fs, dest, _row_tile(n, 256))

    def heads(a, b, t):
        return a.reshape(b, t, N_HEADS, HEAD_DIM)

    u_p = z[:n_p, COL_U:COL_U + D_POOL].reshape(bp, tp, D_POOL)
    u_s = z[n_p:, COL_U:COL_U + D_POOL].reshape(bs, ts, D_POOL)
    pool_state_s = jnp.concatenate([c_pool, u_s], axis=1)[:, ts:]
    outs_p = (heads(z[:n_p, COL_KSB:COL_KSB + D_ATT], bp, tp), heads(z[:n_p, COL_VSB:COL_VSB + D_ATT], bp, tp),
              heads(kn[:n_p], bp, tp), heads(z[:n_p, COL_VFX:COL_VFX + D_ATT], bp, tp),
              logf[:n_p, :N_HEADS].reshape(bp, tp, N_HEADS), u_p[:, tp - POOL_BUF:])
    outs_s = (heads(z[n_p:, COL_KSB:COL_KSB + D_ATT], bs, ts), heads(z[n_p:, COL_VSB:COL_VSB + D_ATT], bs, ts),
              heads(kn[n_p:], bs, ts), heads(z[n_p:, COL_VFX:COL_VFX + D_ATT], bs, ts),
              logf[n_p:, :N_HEADS].reshape(bs, ts, N_HEADS), pool_state_s)
    return y, outs_p, outs_s


def kernel(x_prompt, x_sample, cache_sb_k, cache_sb_v, cache_fox_k, cache_fox_v, cache_fox_logf, state_pool, norm1_g, w_in, b_forget, fox_qnorm_g, fox_knorm_g, pool_w, pool_scale, w_out, norm2_g, router_group_w, router_group_b, router_expert_w, router_expert_b, expert_w_gate, expert_w_up, expert_w_down):
    bp, tp, d = x_prompt.shape
    bs, ts, _ = x_sample.shape
    depth = w_in.shape[0]
    past = cache_sb_k.shape[2]
    n_p, n_s = bp * tp, bs * ts
    dims = (n_p, n_s, bp, tp, bs, ts, past)

    x = jnp.concatenate([x_prompt.reshape(n_p, d), x_sample.reshape(n_s, d)], axis=0)
    w_in_b = jnp.pad(w_in.astype(BF16), ((0, 0), (0, 0), (0, N_IN_PAD - N_IN)))
    w_out_b = w_out.astype(BF16)
    wg_b = expert_w_gate.astype(BF16)
    wu_b = expert_w_up.astype(BF16)
    wd_b = expert_w_down.astype(BF16)
    n_r = N_GROUPS + N_EXPERTS
    w_router = jnp.pad(jnp.concatenate([router_group_w, router_expert_w], axis=-1), ((0, 0), (0, 0), (0, LANE - n_r)))
    b_router = jnp.pad(jnp.concatenate([router_group_b, router_expert_b], axis=-1), ((0, 0), (0, LANE - n_r)))
    b_forget_pad = jnp.pad(b_forget, ((0, 0), (0, LANE - N_HEADS)))

    acc_p = [[] for _ in range(6)]
    acc_s = [[] for _ in range(6)]
    for l in range(depth):
        w = dict(norm1_g=norm1_g[l], w_in=w_in_b[l], b_forget=b_forget_pad[l:l + 1],
                 fox_qnorm_g=fox_qnorm_g[l], fox_knorm_g=fox_knorm_g[l], pool_w=pool_w[l], pool_scale=pool_scale[l],
                 w_out=w_out_b[l], norm2_g=norm2_g[l], w_router=w_router[l], b_router=b_router[l:l + 1],
                 w_gate=wg_b[l], w_up=wu_b[l], w_down=wd_b[l])
        cache = (cache_sb_k[l].reshape(bs * past, D_ATT), cache_sb_v[l].reshape(bs * past, D_ATT),
                 cache_fox_k[l].reshape(bs * past, D_ATT), cache_fox_v[l].reshape(bs * past, D_ATT),
                 cache_fox_logf[l], state_pool[l])
        x, outs_p, outs_s = _layer(x, cache, w, dims)
        for i in range(6):
            acc_p[i].append(outs_p[i])
            acc_s[i].append(outs_s[i])

    y_p = x[:n_p].reshape(bp, tp, d)
    y_s = x[n_p:].reshape(bs, ts, d)
    return (y_p, y_s) + tuple(jnp.stack(a) for a in acc_p) + tuple(jnp.stack(a) for a in acc_s)
```

```python
import functools

import jax
import jax.numpy as jnp
from jax import lax
from jax.experimental import pallas as pl
from jax.experimental.pallas import tpu as pltpu

F32 = jnp.float32
BF16 = jnp.bfloat16

D_MODEL = 4096
HEAD_DIM = 128
N_HEADS = 12
D_ATT = N_HEADS * HEAD_DIM
D_POOL = 1024
POOL_WINDOWS = (2, 4, 8, 16)
POOL_GROUP = 256
POOL_BUF = 15
POOL_HIST = 16
N_GROUPS = 4
EXPERTS_PER_GROUP = 4
N_EXPERTS = 16
N_PAIRS = 6
N_CLASSES = N_GROUPS * N_PAIRS
D_EXPERT = 512
N_IN = 3 * D_ATT + D_POOL + 3 * D_ATT + N_HEADS
LANE = 128
N_IN_PAD = ((N_IN + LANE - 1) // LANE) * LANE
SCALE = HEAD_DIM ** -0.5
EPS = 1e-6
NEG = -1e30
SB_CUT = 110.0

COL_QSB, COL_KSB, COL_VSB = 0, D_ATT, 2 * D_ATT
COL_U = 3 * D_ATT
COL_QFX = COL_U + D_POOL
COL_KFX = COL_QFX + D_ATT
COL_VFX = COL_KFX + D_ATT
COL_F = COL_VFX + D_ATT

HEAD_BLK = 4
PROJ_COLS = HEAD_BLK * HEAD_DIM
ATT_HEADS = 2
VMEM_LIMIT = 56 * 1024 * 1024
MOE_TILE = 256


def _params(sem, vmem=VMEM_LIMIT):
    return pltpu.CompilerParams(dimension_semantics=sem, vmem_limit_bytes=vmem)


def _row_tile(n, target):
    best = 16
    for t in range(16, target + 1, 16):
        if n % t == 0:
            best = t
    assert n % best == 0
    return best


def _softplus(x):
    return jnp.maximum(x, 0.0) + jnp.log(1.0 + jnp.exp(-jnp.abs(x)))


def _split_bf16(x):
    hi = x.astype(BF16)
    lo = (x - hi.astype(F32)).astype(BF16)
    return hi, lo


def _head_rmsnorm(x, g):
    return (x * lax.rsqrt(jnp.mean(x * x, axis=-1, keepdims=True) + EPS)) * g


def _rmsnorm_kernel(x_ref, g_ref, o_ref):
    x = x_ref[...]
    ms = jnp.mean(x * x, axis=-1, keepdims=True)
    o_ref[...] = ((x * lax.rsqrt(ms + EPS)) * g_ref[...]).astype(o_ref.dtype)


def rmsnorm_bf16(x, g):
    n, d = x.shape
    tm = _row_tile(n, 256)
    return pl.pallas_call(
        _rmsnorm_kernel,
        out_shape=jax.ShapeDtypeStruct((n, d), BF16),
        grid=(n // tm,),
        in_specs=[pl.BlockSpec((tm, d), lambda i: (i, 0)), pl.BlockSpec((1, d), lambda i: (0, 0))],
        out_specs=pl.BlockSpec((tm, d), lambda i: (i, 0)),
        compiler_params=_params(("parallel",)),
        name="rmsnorm",
    )(x, g.reshape(1, d))


def _head_proj_kernel(*refs, nb, tm, fox, n_prev):
    x_ref, wq_ref, wk_ref, wv_ref = refs[:4]
    pos = 4
    if fox:
        gq_ref, gk_ref = refs[4:6]
        pos = 6
    pos += n_prev
    q_out, k_out, v_out = refs[pos:pos + 3]
    x = x_ref[...]

    def store(out, r, fn):
        for hh in range(HEAD_BLK):
            slab = fn(r[:, hh * HEAD_DIM:(hh + 1) * HEAD_DIM]).astype(out.dtype)
            for bb in range(nb):
                out[bb, hh] = slab[bb * tm:(bb + 1) * tm, :]

    q = jnp.dot(x, wq_ref[...], preferred_element_type=F32)
    k = jnp.dot(x, wk_ref[...], preferred_element_type=F32)
    v = jnp.dot(x, wv_ref[...], preferred_element_type=F32)
    if fox:
        gq = gq_ref[...] * SCALE
        gk = gk_ref[...]
        store(q_out, q, lambda t: _head_rmsnorm(t, gq))
        store(k_out, k, lambda t: _head_rmsnorm(t, gk))
    else:
        store(q_out, q * SCALE, lambda t: t)
        store(k_out, k, lambda t: t)
    store(v_out, v, lambda t: t)


def head_proj(xn, w, layer, depth, *, row0, n_batch, seq, col_q, fox, prev, fox_params=None):
    tm = min(seq, 512)
    nb = 1 if tm < seq else min(n_batch, max(1, 512 // seq))
    rows = nb * tm
    assert seq % tm == 0 and n_batch % nb == 0 and row0 % rows == 0
    rb0 = row0 // rows
    nt = seq // tm
    hb = HEAD_DIM
    cb = col_q // PROJ_COLS
    nhb = N_HEADS // HEAD_BLK

    def wspec(off):
        return pl.BlockSpec((D_MODEL, PROJ_COLS), lambda j, b, i: (0, cb + off + j))

    in_specs = [pl.BlockSpec((rows, D_MODEL), lambda j, b, i: (rb0 + b * nt + i, 0)),
                wspec(0), wspec(nhb), wspec(2 * nhb)]
    args = [xn, w, w, w]
    if fox:
        gq, gk = fox_params
        in_specs += [pl.BlockSpec((1, hb), lambda j, b, i: (0, 0)),
                     pl.BlockSpec((1, hb), lambda j, b, i: (0, 0))]
        args += [gq.reshape(1, hb), gk.reshape(1, hb)]
    stacked = jax.ShapeDtypeStruct((depth, n_batch, N_HEADS, seq, hb), F32)
    out_shape = [jax.ShapeDtypeStruct((n_batch, N_HEADS, seq, hb), BF16), stacked, stacked]
    out_specs = [pl.BlockSpec((nb, HEAD_BLK, tm, hb), lambda j, b, i: (b, j, i, 0)),
                 pl.BlockSpec((None, nb, HEAD_BLK, tm, hb), lambda j, b, i: (layer, b, j, i, 0)),
                 pl.BlockSpec((None, nb, HEAD_BLK, tm, hb), lambda j, b, i: (layer, b, j, i, 0))]
    n_prev = 2
    aliases = {len(args): 1, len(args) + 1: 2}
    in_specs += [pl.BlockSpec(memory_space=pl.ANY)] * 2
    args += list(prev)
    return pl.pallas_call(
        functools.partial(_head_proj_kernel, nb=nb, tm=tm, fox=fox, n_prev=n_prev),
        out_shape=tuple(out_shape),
        grid=(nhb, n_batch // nb, nt),
        in_specs=in_specs,
        out_specs=tuple(out_specs),
        input_output_aliases=aliases,
        compiler_params=_params(("arbitrary", "arbitrary", "arbitrary")),
        name="head_proj_fox" if fox else "head_proj_sb",
    )(*args)


def _mm_kernel(a_ref, w_ref, o_ref):
    o_ref[...] = jnp.dot(a_ref[...], w_ref[...], preferred_element_type=F32)


def matmul_cols(a, w, *, col0, n_cols, tn, tm_target, name):
    m, k = a.shape
    tm = _row_tile(m, tm_target)
    assert n_cols % tn == 0 and col0 % tn == 0
    cb = col0 // tn
    return pl.pallas_call(
        _mm_kernel,
        out_shape=jax.ShapeDtypeStruct((m, n_cols), F32),
        grid=(n_cols // tn, m // tm),
        in_specs=[pl.BlockSpec((tm, k), lambda j, i: (i, 0)), pl.BlockSpec((k, tn), lambda j, i: (0, cb + j))],
        out_specs=pl.BlockSpec((tm, tn), lambda j, i: (i, j)),
        compiler_params=_params(("arbitrary", "arbitrary")),
        name=name,
    )(a, w)


def _forget_kernel(a_ref, w_ref, b_ref, o_ref):
    f = jnp.dot(a_ref[...], w_ref[...], preferred_element_type=F32) + b_ref[...]
    o_ref[...] = -_softplus(-f)


def forget_gates(xn, w, b_forget):
    n, k = xn.shape
    tm = _row_tile(n, 544)
    return pl.pallas_call(
        _forget_kernel,
        out_shape=jax.ShapeDtypeStruct((n, LANE), F32),
        grid=(n // tm,),
        in_specs=[pl.BlockSpec((tm, k), lambda i: (i, 0)),
                  pl.BlockSpec((k, LANE), lambda i: (0, COL_F // LANE)),
                  pl.BlockSpec((1, LANE), lambda i: (0, 0))],
        out_specs=pl.BlockSpec((tm, LANE), lambda i: (i, 0)),
        compiler_params=_params(("parallel",)),
        name="forget_gates",
    )(xn, w, b_forget)


def _out_proj_kernel(sb_ref, pool_ref, fx_ref, w_ref, r_ref, o_ref):
    acc = r_ref[...] + jnp.dot(sb_ref[...], w_ref[0:D_ATT, :], preferred_element_type=F32)
    acc = acc + jnp.dot(pool_ref[...], w_ref[D_ATT:D_ATT + D_POOL, :], preferred_element_type=F32)
    o_ref[...] = acc + jnp.dot(fx_ref[...], w_ref[D_ATT + D_POOL:, :], preferred_element_type=F32)


def out_proj(o_sb, o_pool, o_fx, w_out, x):
    n = x.shape[0]
    tm = _row_tile(n, 544)
    tn = 1024
    return pl.pallas_call(
        _out_proj_kernel,
        out_shape=jax.ShapeDtypeStruct((n, D_MODEL), F32),
        grid=(n // tm, D_MODEL // tn),
        in_specs=[pl.BlockSpec((tm, D_ATT), lambda i, j: (i, 0)),
                  pl.BlockSpec((tm, D_POOL), lambda i, j: (i, 0)),
                  pl.BlockSpec((tm, D_ATT), lambda i, j: (i, 0)),
                  pl.BlockSpec((D_MODEL, tn), lambda i, j: (0, j)),
                  pl.BlockSpec((tm, tn), lambda i, j: (i, j))],
        out_specs=pl.BlockSpec((tm, tn), lambda i, j: (i, j)),
        compiler_params=_params(("parallel", "arbitrary")),
        name="out_proj",
    )(o_sb, o_pool, o_fx, w_out, x)


def _cumsum_kernel(x_ref, tri_ref, o_ref, *, chunk):
    tri = tri_ref[...]
    carry = jnp.zeros((1, x_ref.shape[1]), F32)
    for c in range(x_ref.shape[0] // chunk):
        x = x_ref[c * chunk:(c + 1) * chunk, :]
        hi = x.astype(BF16)
        r1 = x - hi.astype(F32)
        mid = r1.astype(BF16)
        lo = (r1 - mid.astype(F32)).astype(BF16)
        cs = (jnp.dot(tri, hi, preferred_element_type=F32)
              + jnp.dot(tri, mid, preferred_element_type=F32)
              + jnp.dot(tri, lo, preferred_element_type=F32)) + carry
        o_ref[c * chunk:(c + 1) * chunk, :] = cs
        carry = cs[chunk - 1:chunk, :]


def cumsum_time(x):
    b, length, w = x.shape
    chunk = next(c for c in (256, 128, 64, 32, 16, 8) if length % c == 0)
    tri = jnp.tril(jnp.ones((chunk, chunk), BF16))
    return pl.pallas_call(
        functools.partial(_cumsum_kernel, chunk=chunk),
        out_shape=jax.ShapeDtypeStruct((b, length, w), F32),
        grid=(b,),
        in_specs=[pl.BlockSpec((None, length, w), lambda i: (i, 0, 0)),
                  pl.BlockSpec((chunk, chunk), lambda i: (0, 0))],
        out_specs=pl.BlockSpec((None, length, w), lambda i: (i, 0, 0)),
        compiler_params=_params(("parallel",)),
        name="cumsum_time",
    )(x, tri)


def _sb_block(q, k, v, tri, r_in, before):
    zz = lax.dot_general(q, k, (((1,), (1,)), ((), ())), preferred_element_type=F32)
    lk = -_softplus(zz)
    if before is not None:
        lk = jnp.where(before, lk, 0.0)
    hi, lo = _split_bf16(lk)
    rc = jnp.dot(hi, tri, preferred_element_type=F32) + jnp.dot(lo, tri, preferred_element_type=F32)
    e = jnp.exp(zz + rc + r_in)
    if before is not None:
        e = jnp.where(before, e, 0.0)
    pv = jnp.dot(e.astype(BF16), v, preferred_element_type=F32)
    return pv, r_in + rc[:, 0:1]


def _sb_sweep(q_of, k_of, v_of, tri, acc, r, j_start, nh):
    def live(rs):
        m = jnp.max(rs[0])
        for hh in range(1, nh):
            m = jnp.maximum(m, jnp.max(rs[hh]))
        return m > -SB_CUT

    def cond(carry):
        j, _, rs = carry
        return jnp.logical_and(j >= 0, live(rs))

    def body(carry):
        j, accs, rs = carry
        new_acc, new_r = [], []
        for hh in range(nh):
            pv, r2 = _sb_block(q_of(hh), k_of(hh, j), v_of(hh, j), tri, rs[hh], None)
            new_acc.append(accs[hh] + pv)
            new_r.append(r2)
        return j - 1, tuple(new_acc), tuple(new_r)

    _, acc, _ = lax.while_loop(cond, body, (j_start, tuple(acc), tuple(r)))
    return acc


def _sb_prompt_kernel(q_ref, k_ref, v_ref, tri_ref, prev_ref, o_ref, kb, vb, *, tq, nk, nh):
    qi = pl.program_id(2)

    @pl.when(qi == 0)
    def _():
        for hh in range(nh):
            for c in range(nk):
                kb[hh, c] = k_ref[hh, c * tq:(c + 1) * tq, :].astype(BF16)
                vb[hh, c] = v_ref[hh, c * tq:(c + 1) * tq, :].astype(BF16)

    tri = tri_ref[...]
    row = lax.broadcasted_iota(jnp.int32, (tq, tq), 0)
    col = lax.broadcasted_iota(jnp.int32, (tq, tq), 1)
    before = col < row
    acc, r = [], []
    for hh in range(nh):
        a0, r0 = _sb_block(q_ref[hh], kb[hh, qi], vb[hh, qi], tri, jnp.zeros((tq, 1), F32), before)
        acc.append(a0)
        r.append(r0)
    acc = _sb_sweep(lambda hh: q_ref[hh], lambda hh, j: kb[hh, j], lambda hh, j: vb[hh, j], tri, acc, r, qi - 1, nh)
    for hh in range(nh):
        o_ref[:, hh * HEAD_DIM:(hh + 1) * HEAD_DIM] = acc[hh].astype(o_ref.dtype)


def sb_attention_prompt(q, k_all, v_all, layer, o_prev, tq=256):
    n_batch, _, seq, hb = q.shape
    nq = seq // tq
    nh = ATT_HEADS
    tri = jnp.tril(jnp.ones((tq, tq), BF16))
    return pl.pallas_call(
        functools.partial(_sb_prompt_kernel, tq=tq, nk=nq, nh=nh),
        out_shape=jax.ShapeDtypeStruct(o_prev.shape, o_prev.dtype),
        grid=(n_batch, N_HEADS // nh, nq),
        in_specs=[pl.BlockSpec((None, nh, tq, hb), lambda b, h, i: (b, h, i, 0)),
                  pl.BlockSpec((None, None, nh, seq, hb), lambda b, h, i: (layer, b, h, 0, 0)),
                  pl.BlockSpec((None, None, nh, seq, hb), lambda b, h, i: (layer, b, h, 0, 0)),
                  pl.BlockSpec((tq, tq), lambda b, h, i: (0, 0)),
                  pl.BlockSpec(memory_space=pl.ANY)],
        out_specs=pl.BlockSpec((tq, nh * hb), lambda b, h, i: (b * nq + i, h)),
        input_output_aliases={4: 0},
        scratch_shapes=[pltpu.VMEM((nh, nq, tq, hb), BF16), pltpu.VMEM((nh, nq, tq, hb), BF16)],
        compiler_params=_params(("parallel", "parallel", "arbitrary")),
        name="sb_prompt",
    )(q, k_all, v_all, tri, o_prev)


def _sb_sample_kernel(q_ref, kn_ref, vn_ref, kc_ref, vc_ref, tri_n_ref, tri_c_ref, prev_ref, o_ref, *, ts, past, ck, nh):
    row = lax.broadcasted_iota(jnp.int32, (ts, ts), 0)
    col = lax.broadcasted_iota(jnp.int32, (ts, ts), 1)
    before = col < row
    acc, r = [], []
    for hh in range(nh):
        a0, r0 = _sb_block(q_ref[hh], kn_ref[hh].astype(BF16), vn_ref[hh].astype(BF16), tri_n_ref[...],
                           jnp.zeros((ts, 1), F32), before)
        acc.append(a0)
        r.append(r0)

    def cache_block(ref, hh, j):
        return ref[hh, pl.ds(pl.multiple_of(j * ck, ck), ck), :].astype(BF16)

    acc = _sb_sweep(lambda hh: q_ref[hh], functools.partial(cache_block, kc_ref), functools.partial(cache_block, vc_ref),
                    tri_c_ref[...], acc, r, past // ck - 1, nh)
    for hh in range(nh):
        o_ref[:, hh * HEAD_DIM:(hh + 1) * HEAD_DIM] = acc[hh].astype(o_ref.dtype)


def sb_attention_sample(q, k_all, v_all, cache_k, cache_v, layer, o_prev, row0):
    n_batch, _, ts, hb = q.shape
    past = cache_k.shape[3]
    nh = ATT_HEADS
    ck = next(c for c in (256, 128, 64, 32, 16, 8) if past % c == 0)
    tri_n = jnp.tril(jnp.ones((ts, ts), BF16))
    tri_c = jnp.tril(jnp.ones((ck, ck), BF16))
    rb = row0 // ts

    def spec5(t):
        return pl.BlockSpec((None, None, nh, t, hb), lambda b, h: (layer, b, h, 0, 0))

    return pl.pallas_call(
        functools.partial(_sb_sample_kernel, ts=ts, past=past, ck=ck, nh=nh),
        out_shape=jax.ShapeDtypeStruct(o_prev.shape, o_prev.dtype),
        grid=(n_batch, N_HEADS // nh),
        in_specs=[pl.BlockSpec((None, nh, ts, hb), lambda b, h: (b, h, 0, 0)),
                  spec5(ts), spec5(ts), spec5(past), spec5(past),
                  pl.BlockSpec((ts, ts), lambda b, h: (0, 0)),
                  pl.BlockSpec((ck, ck), lambda b, h: (0, 0)),
                  pl.BlockSpec(memory_space=pl.ANY)],
        out_specs=pl.BlockSpec((ts, nh * hb), lambda b, h: (rb + b, h)),
        input_output_aliases={7: 0},
        compiler_params=_params(("parallel", "parallel")),
        name="sb_sample",
    )(q, k_all, v_all, cache_k, cache_v, tri_n, tri_c, o_prev)


def _fox_prompt_kernel(q_ref, k_ref, v_ref, cq_ref, ck_ref, prev_ref, o_ref, kb, vb, *, tq, nk, nh):
    qi = pl.program_id(2)

    @pl.when(qi == 0)
    def _():
        for hh in range(nh):
            for c in range(nk):
                kb[hh, c] = k_ref[hh, c * tq:(c + 1) * tq, :].astype(BF16)
                vb[hh, c] = v_ref[hh, c * tq:(c + 1) * tq, :].astype(BF16)

    def scores(hh, j):
        s = lax.dot_general(q_ref[hh], kb[hh, j], (((1,), (1,)), ((), ())), preferred_element_type=F32)
        return s + cq_ref[hh] - ck_ref[hh, j]

    row = lax.broadcasted_iota(jnp.int32, (tq, tq), 0)
    col = lax.broadcasted_iota(jnp.int32, (tq, tq), 1)
    init = []
    for hh in range(nh):
        s = jnp.where(col <= row, scores(hh, qi), NEG)
        m0 = jnp.max(s, axis=-1, keepdims=True)
        p = jnp.exp(s - m0)
        l0 = jnp.sum(p, axis=-1, keepdims=True)
        init.append((m0, l0, jnp.dot(p.astype(BF16), vb[hh, qi], preferred_element_type=F32)))

    def body(j, carry):
        out = []
        for hh in range(nh):
            m, l, acc = carry[hh]
            s = scores(hh, j)
            m_new = jnp.maximum(m, jnp.max(s, axis=-1, keepdims=True))
            alpha = jnp.exp(m - m_new)
            p = jnp.exp(s - m_new)
            l = alpha * l + jnp.sum(p, axis=-1, keepdims=True)
            acc = alpha * acc + jnp.dot(p.astype(BF16), vb[hh, j], preferred_element_type=F32)
            out.append((m_new, l, acc))
        return tuple(out)

    fin = lax.fori_loop(0, qi, body, tuple(init))
    for hh in range(nh):
        _, l, acc = fin[hh]
        o_ref[:, hh * HEAD_DIM:(hh + 1) * HEAD_DIM] = (acc / l).astype(o_ref.dtype)


def fox_attention_prompt(q, k_all, v_all, cq, ck, layer, o_prev, tq=256):
    n_batch, _, seq, hb = q.shape
    nq = seq // tq
    nh = ATT_HEADS
    return pl.pallas_call(
        functools.partial(_fox_prompt_kernel, tq=tq, nk=nq, nh=nh),
        out_shape=jax.ShapeDtypeStruct(o_prev.shape, o_prev.dtype),
        grid=(n_batch, N_HEADS // nh, nq),
        in_specs=[pl.BlockSpec((None, nh, tq, hb), lambda b, h, i: (b, h, i, 0)),
                  pl.BlockSpec((None, None, nh, seq, hb), lambda b, h, i: (layer, b, h, 0, 0)),
                  pl.BlockSpec((None, None, nh, seq, hb), lambda b, h, i: (layer, b, h, 0, 0)),
                  pl.BlockSpec((None, nh, tq, 1), lambda b, h, i: (b, h, i, 0)),
                  pl.BlockSpec((None, nh, nq, 1, tq), lambda b, h, i: (b, h, 0, 0, 0)),
                  pl.BlockSpec(memory_space=pl.ANY)],
        out_specs=pl.BlockSpec((tq, nh * hb), lambda b, h, i: (b * nq + i, h)),
        input_output_aliases={5: 0},
        scratch_shapes=[pltpu.VMEM((nh, nq, tq, hb), BF16), pltpu.VMEM((nh, nq, tq, hb), BF16)],
        compiler_params=_params(("parallel", "parallel", "arbitrary")),
        name="fox_prompt",
    )(q, k_all, v_all, cq, ck, o_prev)


def _fox_sample_kernel(q_ref, kn_ref, vn_ref, kc_ref, vc_ref, cq_ref, ckn_ref, ckc_ref, prev_ref, o_ref, *, ts, nh):
    dn = (((1,), (1,)), ((), ()))
    row = lax.broadcasted_iota(jnp.int32, (ts, ts), 0)
    col = lax.broadcasted_iota(jnp.int32, (ts, ts), 1)
    for hh in range(nh):
        q = q_ref[hh]
        cq = cq_ref[hh]
        s_c = lax.dot_general(q, kc_ref[hh].astype(BF16), dn, preferred_element_type=F32) + cq - ckc_ref[hh]
        s_n = lax.dot_general(q, kn_ref[hh].astype(BF16), dn, preferred_element_type=F32) + cq - ckn_ref[hh]
        s_n = jnp.where(col <= row, s_n, NEG)
        m = jnp.maximum(jnp.max(s_c, axis=-1, keepdims=True), jnp.max(s_n, axis=-1, keepdims=True))
        p_c = jnp.exp(s_c - m)
        p_n = jnp.exp(s_n - m)
        l = jnp.sum(p_c, axis=-1, keepdims=True) + jnp.sum(p_n, axis=-1, keepdims=True)
        acc = (jnp.dot(p_c.astype(BF16), vc_ref[hh].astype(BF16), preferred_element_type=F32)
               + jnp.dot(p_n.astype(BF16), vn_ref[hh].astype(BF16), preferred_element_type=F32))
        o_ref[:, hh * HEAD_DIM:(hh + 1) * HEAD_DIM] = (acc / l).astype(o_ref.dtype)


def fox_attention_sample(q, k_all, v_all, cache_k, cache_v, cq, ckn, ckc, layer, o_prev, row0):
    n_batch, _, ts, hb = q.shape
    past = cache_k.shape[3]
    nh = ATT_HEADS
    rb = row0 // ts

    def spec5(t):
        return pl.BlockSpec((None, None, nh, t, hb), lambda b, h: (layer, b, h, 0, 0))

    return pl.pallas_call(
        functools.partial(_fox_sample_kernel, ts=ts, nh=nh),
        out_shape=jax.ShapeDtypeStruct(o_prev.shape, o_prev.dtype),
        grid=(n_batch, N_HEADS // nh),
        in_specs=[pl.BlockSpec((None, nh, ts, hb), lambda b, h: (b, h, 0, 0)),
                  spec5(ts), spec5(ts), spec5(past), spec5(past),
                  pl.BlockSpec((None, nh, ts, 1), lambda b, h: (b, h, 0, 0)),
                  pl.BlockSpec((None, nh, 1, ts), lambda b, h: (b, h, 0, 0)),
                  pl.BlockSpec((None, nh, 1, past), lambda b, h: (b, h, 0, 0)),
                  pl.BlockSpec(memory_space=pl.ANY)],
        out_specs=pl.BlockSpec((ts, nh * hb), lambda b, h: (rb + b, h)),
        input_output_aliases={8: 0},
        compiler_params=_params(("parallel", "parallel")),
        name="fox_sample",
    )(q, k_all, v_all, cache_k, cache_v, cq, ckn, ckc, o_prev)


def _pool_kernel(u_ref, hist_ref, w_ref, sc_ref, prev_ref, o_ref, ext, *, seq, pos0, chunk):
    g = pl.program_id(1)
    ext[0:POOL_HIST, :] = hist_ref[...]
    ext[POOL_HIST:POOL_HIST + seq, :] = u_ref[...]
    w = w_ref[...].astype(BF16)
    sc = sc_ref[...]
    for gi, win in enumerate(POOL_WINDOWS):
        @pl.when(g == gi)
        def _(win=win):
            for c in range(seq // chunk):
                base = POOL_HIST + c * chunk
                ws = ext[base:base + chunk, :]
                for i in range(1, win):
                    ws = ws + ext[base - i:base - i + chunk, :]
                t = lax.broadcasted_iota(jnp.int32, (chunk, 1), 0) + (pos0 + c * chunk + 1)
                cnt = jnp.minimum(win, t).astype(F32)
                d = ws / cnt - ext[base:base + chunk, :]
                y = jnp.dot(d.astype(BF16), w, preferred_element_type=F32) * sc
                o_ref[c * chunk:(c + 1) * chunk, :] = y.astype(o_ref.dtype)


def pool_mixer(u, hist, pool_w, pool_scale, row0, n_batch, seq, pos0, o_prev):
    pg = POOL_GROUP
    chunk = min(seq, 256)
    rb = row0 // seq
    in_specs = [pl.BlockSpec((seq, pg), lambda b, g: (rb + b, g)),
                pl.BlockSpec((None, POOL_HIST, pg), lambda b, g: (b, 0, g)),
                pl.BlockSpec((None, pg, pg), lambda b, g: (g, 0, 0)),
                pl.BlockSpec((1, pg), lambda b, g: (0, g)),
                pl.BlockSpec(memory_space=pl.ANY)]
    args = [u, hist, pool_w, pool_scale.reshape(1, D_POOL), o_prev]
    return pl.pallas_call(
        functools.partial(_pool_kernel, seq=seq, pos0=pos0, chunk=chunk),
        out_shape=jax.ShapeDtypeStruct(o_prev.shape, o_prev.dtype),
        grid=(n_batch, len(POOL_WINDOWS)),
        in_specs=in_specs,
        out_specs=pl.BlockSpec((seq, pg), lambda b, g: (rb + b, g)),
        scratch_shapes=[pltpu.VMEM((POOL_HIST + seq, pg), F32)],
        input_output_aliases={4: 0},
        compiler_params=_params(("parallel", "arbitrary")),
        name="pool_mixer",
    )(*args)


def _router_kernel(h_ref, g_ref, w_ref, b_ref, meta_ref):
    h = h_ref[...]
    xn = (h * lax.rsqrt(jnp.mean(h * h, axis=-1, keepdims=True) + EPS)) * g_ref[...]
    w = w_ref[...]
    xh, xl = _split_bf16(xn)
    wh, wl = _split_bf16(w)
    lg = (jnp.dot(xh, wh, preferred_element_type=F32)
          + (jnp.dot(xh, wl, preferred_element_type=F32) + jnp.dot(xl, wh, preferred_element_type=F32))
          ) + b_ref[...]
    tm = lg.shape[0]
    lane = lax.broadcasted_iota(jnp.int32, (tm, LANE), 1)
    big = jnp.int32(LANE)
    gl = jnp.where(lane < N_GROUPS, lg, NEG)
    gmax = jnp.max(gl, axis=-1, keepdims=True)
    gsel = jnp.min(jnp.where(gl == gmax, lane, big), axis=-1, keepdims=True)
    gsum = jnp.sum(jnp.where(lane < N_GROUPS, jnp.exp(gl - gmax), 0.0), axis=-1, keepdims=True)
    gw = 1.0 / gsum
    lo_lane = N_GROUPS + EXPERTS_PER_GROUP * gsel
    el = jnp.where((lane >= lo_lane) & (lane < lo_lane + EXPERTS_PER_GROUP), lg, NEG)
    t1 = jnp.max(el, axis=-1, keepdims=True)
    i1 = jnp.min(jnp.where(el == t1, lane, big), axis=-1, keepdims=True)
    el2 = jnp.where(lane == i1, NEG, el)
    t2 = jnp.max(el2, axis=-1, keepdims=True)
    i2 = jnp.min(jnp.where(el2 == t2, lane, big), axis=-1, keepdims=True)
    e2 = jnp.exp(t2 - t1)
    den = 1.0 + e2
    w1 = gw * (1.0 / den)
    w2 = gw * (e2 / den)
    a1 = i1 - lo_lane
    a2 = i2 - lo_lane
    first_low = a1 < a2
    e_lo = jnp.where(first_low, a1, a2)
    e_hi = jnp.where(first_low, a2, a1)
    g_lo = jnp.where(first_low, w1, w2)
    g_hi = jnp.where(first_low, w2, w1)
    pair = (e_lo * (7 - e_lo)) // 2 + (e_hi - e_lo - 1)
    cls = (gsel * N_PAIRS + pair).astype(F32)
    meta_ref[...] = jnp.where(lane == 0, g_lo, jnp.where(lane == 1, g_hi, jnp.where(lane == 2, cls, 0.0)))


def router(h, g2, w_router, b_router):
    n, d = h.shape
    tm = _row_tile(n, 256)
    return pl.pallas_call(
        _router_kernel,
        out_shape=jax.ShapeDtypeStruct((n, LANE), F32),
        grid=(n // tm,),
        in_specs=[pl.BlockSpec((tm, d), lambda i: (i, 0)),
                  pl.BlockSpec((1, d), lambda i: (0, 0)),
                  pl.BlockSpec((d, LANE), lambda i: (0, 0)),
                  pl.BlockSpec((1, LANE), lambda i: (0, 0))],
        out_specs=pl.BlockSpec((tm, LANE), lambda i: (i, 0)),
        compiler_params=_params(("parallel",)),
        name="router",
    )(h, g2.reshape(1, d), w_router, b_router)


def _row_copy(src_hbm, dst_vmem, sem, src_row, dst_row):
    return pltpu.make_async_copy(src_hbm.at[pl.ds(src_row, 1), :], dst_vmem.at[pl.ds(dst_row, 1), :], sem)


def _gather_rows(idx_ref, base, src_hbm, dst_vmem, sem, n_rows):
    def issue(r, c):
        _row_copy(src_hbm, dst_vmem, sem, idx_ref[base + r], r).start()
        return c

    lax.fori_loop(0, n_rows, issue, 0)
    pltpu.make_async_copy(src_hbm.at[pl.ds(0, n_rows), :], dst_vmem, sem).wait()


def _gather_kernel(idx_ref, src_hbm, o_ref, sem, *, tm):
    _gather_rows(idx_ref, pl.program_id(0) * tm, src_hbm, o_ref, sem, tm)


def gather_rows(src, idx, tm):
    n_out = idx.shape[0]
    d = src.shape[1]
    return pl.pallas_call(
        functools.partial(_gather_kernel, tm=tm),
        out_shape=jax.ShapeDtypeStruct((n_out, d), src.dtype),
        grid_spec=pltpu.PrefetchScalarGridSpec(
            num_scalar_prefetch=1,
            grid=(n_out // tm,),
            in_specs=[pl.BlockSpec(memory_space=pl.ANY)],
            out_specs=pl.BlockSpec((tm, d), lambda t, idx: (t, 0)),
            scratch_shapes=[pltpu.SemaphoreType.DMA(())]),
        compiler_params=_params(("arbitrary",)),
        name="gather_rows",
    )(idx, src)


def _combine_kernel(idx_ref, h_ref, f_hbm, o_ref, buf, sem, *, tm):
    _gather_rows(idx_ref, pl.program_id(0) * tm, f_hbm, buf, sem, tm)
    o_ref[...] = h_ref[...] + buf[...]


def combine_rows(h, f_sorted, dest, tm):
    n, d = h.shape
    return pl.pallas_call(
        functools.partial(_combine_kernel, tm=tm),
        out_shape=jax.ShapeDtypeStruct((n, d), F32),
        grid_spec=pltpu.PrefetchScalarGridSpec(
            num_scalar_prefetch=1,
            grid=(n // tm,),
            in_specs=[pl.BlockSpec((tm, d), lambda t, idx: (t, 0)), pl.BlockSpec(memory_space=pl.ANY)],
            out_specs=pl.BlockSpec((tm, d), lambda t, idx: (t, 0)),
            scratch_shapes=[pltpu.VMEM((tm, d), F32), pltpu.SemaphoreType.DMA(())]),
        compiler_params=_params(("arbitrary",)),
        name="combine_rows",
    )(dest, h, f_sorted)


def _moe_kernel(texp_ref, tsel_ref, tvalid_ref, hs_ref, gate_ref, g2_ref, wg_ref, wu_ref, wd_ref, o_ref, xn):
    t = pl.program_id(0)
    k = pl.program_id(1)
    valid = tvalid_ref[t] == 1

    @pl.when(jnp.logical_and(valid, k == 0))
    def _():
        h = hs_ref[...]
        y = (h * lax.rsqrt(jnp.mean(h * h, axis=-1, keepdims=True) + EPS)) * g2_ref[...]
        xn[...] = y.astype(xn.dtype)

    @pl.when(valid)
    def _():
        x = xn[...]
        g = jnp.dot(x, wg_ref[...], preferred_element_type=F32)
        u = jnp.dot(x, wu_ref[...], preferred_element_type=F32)
        gates = gate_ref[...]
        gate = jnp.where(tsel_ref[2 * t + k] == 0, gates[:, 0:1], gates[:, 1:2])
        hm = ((g * (1.0 / (1.0 + jnp.exp(-g)))) * u) * gate
        contrib = jnp.dot(hm.astype(BF16), wd_ref[...], preferred_element_type=F32)

        @pl.when(k == 0)
        def _():
            o_ref[...] = contrib

        @pl.when(k == 1)
        def _():
            o_ref[...] = o_ref[...] + contrib

    @pl.when(jnp.logical_and(jnp.logical_not(valid), k == 0))
    def _():
        o_ref[...] = jnp.zeros_like(o_ref)


def moe_sorted(hs, gates_sorted, g2, wg, wu, wd, texp, tsel, tvalid, tm):
    npad, d = hs.shape
    n_tiles = npad // tm
    return pl.pallas_call(
        _moe_kernel,
        out_shape=jax.ShapeDtypeStruct((npad, d), F32),
        grid_spec=pltpu.PrefetchScalarGridSpec(
            num_scalar_prefetch=3,
            grid=(n_tiles, 2),
            in_specs=[pl.BlockSpec((tm, d), lambda t, k, te, ts, tv: (t, 0)),
                      pl.BlockSpec((tm, LANE), lambda t, k, te, ts, tv: (t, 0)),
                      pl.BlockSpec((1, d), lambda t, k, te, ts, tv: (0, 0)),
                      pl.BlockSpec((None, d, D_EXPERT), lambda t, k, te, ts, tv: (te[2 * t + k], 0, 0)),
                      pl.BlockSpec((None, d, D_EXPERT), lambda t, k, te, ts, tv: (te[2 * t + k], 0, 0)),
                      pl.BlockSpec((None, D_EXPERT, d), lambda t, k, te, ts, tv: (te[2 * t + k], 0, 0))],
            out_specs=pl.BlockSpec((tm, d), lambda t, k, te, ts, tv: (t, 0)),
            scratch_shapes=[pltpu.VMEM((tm, d), BF16)]),
        compiler_params=_params(("arbitrary", "arbitrary")),
        name="moe_sorted",
    )(texp, tsel, tvalid, hs, gates_sorted, g2.reshape(1, d), wg, wu, wd)


_PAIR_LO = (0, 0, 0, 1, 1, 2)
_PAIR_HI = (1, 2, 3, 2, 3, 3)


def _route_tables(cls, tm):
    n = cls.shape[0]
    n_tiles = n // tm + N_CLASSES
    onehot = (cls[:, None] == jnp.arange(N_CLASSES, dtype=jnp.int32)[None, :]).astype(jnp.int32)
    csum = jnp.cumsum(onehot, axis=0)
    counts = csum[-1]
    rank = jnp.sum(csum * onehot, axis=1) - 1
    padded = ((counts + tm - 1) // tm) * tm
    ends = jnp.cumsum(padded)
    starts = ends - padded
    dest = (jnp.sum(starts[None, :] * onehot, axis=1) + rank).astype(jnp.int32)
    src = jnp.zeros((n_tiles * tm,), jnp.int32).at[dest].set(jnp.arange(n, dtype=jnp.int32))
    n_used = ends[-1] // tm
    tile = jnp.arange(n_tiles, dtype=jnp.int32)
    tvalid = (tile < n_used).astype(jnp.int32)
    tile_c = jnp.minimum(tile, n_used - 1)
    tile_oh = jnp.logical_and(tile_c[:, None] * tm >= starts[None, :], tile_c[:, None] * tm < ends[None, :]).astype(jnp.int32)
    cls_id = jnp.arange(N_CLASSES, dtype=jnp.int32)
    group = jnp.sum(tile_oh * (cls_id // N_PAIRS)[None, :], axis=1)
    pair_lo = jnp.sum(tile_oh * jnp.asarray(_PAIR_LO * N_GROUPS, jnp.int32)[None, :], axis=1)
    pair_hi = jnp.sum(tile_oh * jnp.asarray(_PAIR_HI * N_GROUPS, jnp.int32)[None, :], axis=1)
    tile_start = jnp.sum(tile_oh * starts[None, :], axis=1)
    e_lo = group * EXPERTS_PER_GROUP + pair_lo
    e_hi = group * EXPERTS_PER_GROUP + pair_hi
    flip = ((tile_c * tm - tile_start) // tm) % 2
    first = jnp.where(flip == 1, e_hi, e_lo)
    second = jnp.where(flip == 1, e_lo, e_hi)
    texp = jnp.stack([first, second], axis=1).reshape(-1).astype(jnp.int32)
    tsel = jnp.stack([flip, 1 - flip], axis=1).reshape(-1).astype(jnp.int32)
    return dest, src, texp, tsel, tvalid


def _head_major(a):
    return jnp.swapaxes(a, -3, -2)


def _layer(x, layer, depth, cache, w, dims, prev):
    n_p, n_s, bp, tp, bs, ts, past = dims
    n = n_p + n_s
    c_sbk, c_sbv, c_fxk, c_fxv, c_logf, c_pool = cache

    xn = rmsnorm_bf16(x, w["norm1_g"])
    fox_params = (w["fox_qnorm_g"], w["fox_knorm_g"])
    pv = prev
    q_sb_p, ksb_p, vsb_p = head_proj(xn, w["w_in"], layer, depth, row0=0, n_batch=bp, seq=tp, col_q=COL_QSB,
                                     fox=False, prev=pv["sb_p"])
    q_sb_s, ksb_s, vsb_s = head_proj(xn, w["w_in"], layer, depth, row0=n_p, n_batch=bs, seq=ts, col_q=COL_QSB,
                                     fox=False, prev=pv["sb_s"])
    q_fx_p, kfx_p, vfx_p = head_proj(xn, w["w_in"], layer, depth, row0=0, n_batch=bp, seq=tp, col_q=COL_QFX,
                                     fox=True, fox_params=fox_params, prev=pv["fx_p"])
    q_fx_s, kfx_s, vfx_s = head_proj(xn, w["w_in"], layer, depth, row0=n_p, n_batch=bs, seq=ts, col_q=COL_QFX,
                                     fox=True, fox_params=fox_params, prev=pv["fx_s"])
    lf = forget_gates(xn, w["w_in"], w["b_forget"])
    lf_p, lf_s = lf[:n_p], lf[n_p:]
    u = matmul_cols(xn, w["w_in"], col0=COL_U, n_cols=D_POOL, tn=512, tm_target=544, name="pool_proj")

    c_p = cumsum_time(lf_p.reshape(bp, tp, LANE))[:, :, :N_HEADS]
    lf_cat = jnp.concatenate([jnp.pad(c_logf, ((0, 0), (0, 0), (0, LANE - N_HEADS))),
                              lf_s.reshape(bs, ts, LANE)], axis=1)
    c_s = cumsum_time(lf_cat)[:, :, :N_HEADS]

    tq = min(256, tp)
    c_p_t = jnp.transpose(c_p, (0, 2, 1))
    cq_p = c_p_t[..., None]
    ck_p = c_p_t.reshape(bp, N_HEADS, tp // tq, 1, tq)
    c_s_t = jnp.transpose(c_s, (0, 2, 1))
    cq_s = c_s_t[:, :, past:, None]
    ckn_s = c_s_t[:, :, None, past:]
    ckc_s = c_s_t[:, :, None, :past]

    o_sb = sb_attention_prompt(q_sb_p, ksb_p, vsb_p, layer, jnp.zeros((n, D_ATT), BF16), tq)
    o_sb = sb_attention_sample(q_sb_s, ksb_s, vsb_s, c_sbk, c_sbv, layer, o_sb, n_p)
    o_fx = fox_attention_prompt(q_fx_p, kfx_p, vfx_p, cq_p, ck_p, layer, jnp.zeros((n, D_ATT), BF16), tq)
    o_fx = fox_attention_sample(q_fx_s, kfx_s, vfx_s, c_fxk, c_fxv, cq_s, ckn_s, ckc_s, layer, o_fx, n_p)
    pool_w = w["pool_w"]
    o_pl = pool_mixer(u, jnp.zeros((bp, POOL_HIST, D_POOL), F32), pool_w, w["pool_scale"], 0, bp, tp, 0,
                      jnp.zeros((n, D_POOL), BF16))
    hist_s = jnp.pad(c_pool, ((0, 0), (POOL_HIST - POOL_BUF, 0), (0, 0)))
    o_pl = pool_mixer(u, hist_s, pool_w, w["pool_scale"], n_p, bs, ts, past, o_pl)

    h = out_proj(o_sb, o_pl, o_fx, w["w_out"], x)

    meta = router(h, w["norm2_g"], w["w_router"], w["b_router"])
    cls = meta[:, 2].astype(jnp.int32)
    dest, src, texp, tsel, tvalid = _route_tables(cls, MOE_TILE)
    hs = gather_rows(h, src, MOE_TILE)
    gates_sorted = jnp.take(meta, src, axis=0)
    fs = moe_sorted(hs, gates_sorted, w["norm2_g"], w["w_gate"], w["w_up"], w["w_down"],
                    texp, tsel, tvalid, MOE_TILE)
    y = combine_rows(h, fs, dest, _row_tile(n, 256))

    u_tail_p = jnp.stack([u[(b + 1) * tp - POOL_BUF:(b + 1) * tp] for b in range(bp)])
    u_s = u[n_p:].reshape(bs, ts, D_POOL)
    small = (lf_p[:, :N_HEADS].reshape(bp, tp, N_HEADS), u_tail_p,
             lf_s[:, :N_HEADS].reshape(bs, ts, N_HEADS), jnp.concatenate([c_pool, u_s], axis=1)[:, ts:])
    new_prev = dict(sb_p=(ksb_p, vsb_p), sb_s=(ksb_s, vsb_s), fx_p=(kfx_p, vfx_p), fx_s=(kfx_s, vfx_s))
    return y, new_prev, small


def kernel(x_prompt, x_sample, cache_sb_k, cache_sb_v, cache_fox_k, cache_fox_v, cache_fox_logf, state_pool, norm1_g, w_in, b_forget, fox_qnorm_g, fox_knorm_g, pool_w, pool_scale, w_out, norm2_g, router_group_w, router_group_b, router_expert_w, router_expert_b, expert_w_gate, expert_w_up, expert_w_down):
    bp, tp, d = x_prompt.shape
    bs, ts, _ = x_sample.shape
    depth = w_in.shape[0]
    past = cache_sb_k.shape[2]
    n_p, n_s = bp * tp, bs * ts
    dims = (n_p, n_s, bp, tp, bs, ts, past)

    x = jnp.concatenate([x_prompt.reshape(n_p, d), x_sample.reshape(n_s, d)], axis=0)
    w_in_b = jnp.pad(w_in.astype(BF16), ((0, 0), (0, 0), (0, N_IN_PAD - N_IN)))
    w_out_b = w_out.astype(BF16)
    wg_b = expert_w_gate.astype(BF16)
    wu_b = expert_w_up.astype(BF16)
    wd_b = expert_w_down.astype(BF16)
    n_r = N_GROUPS + N_EXPERTS
    w_router = jnp.pad(jnp.concatenate([router_group_w, router_expert_w], axis=-1), ((0, 0), (0, 0), (0, LANE - n_r)))
    b_router = jnp.pad(jnp.concatenate([router_group_b, router_expert_b], axis=-1), ((0, 0), (0, LANE - n_r)))
    b_forget_pad = jnp.pad(b_forget, ((0, 0), (0, LANE - N_HEADS)))
    caches = tuple(_head_major(c) for c in (cache_sb_k, cache_sb_v, cache_fox_k, cache_fox_v))

    def zeros_kv(nb, t):
        return (jnp.zeros((depth, nb, N_HEADS, t, HEAD_DIM), F32), jnp.zeros((depth, nb, N_HEADS, t, HEAD_DIM), F32))

    prev = dict(sb_p=zeros_kv(bp, tp), sb_s=zeros_kv(bs, ts), fx_p=zeros_kv(bp, tp), fx_s=zeros_kv(bs, ts))
    small_acc = [[] for _ in range(4)]
    for l in range(depth):
        w = dict(norm1_g=norm1_g[l], w_in=w_in_b[l], b_forget=b_forget_pad[l:l + 1],
                 fox_qnorm_g=fox_qnorm_g[l], fox_knorm_g=fox_knorm_g[l], pool_w=pool_w[l], pool_scale=pool_scale[l],
                 w_out=w_out_b[l], norm2_g=norm2_g[l], w_router=w_router[l], b_router=b_router[l:l + 1],
                 w_gate=wg_b[l], w_up=wu_b[l], w_down=wd_b[l])
        cache = caches + (cache_fox_logf[l], state_pool[l])
        x, prev, small = _layer(x, l, depth, cache, w, dims, prev)
        for i in range(4):
            small_acc[i].append(small[i])

    y_p = x[:n_p].reshape(bp, tp, d)
    y_s = x[n_p:].reshape(bs, ts, d)
    logf_p, pool_p, logf_s, pool_s = (jnp.stack(a) for a in small_acc)
    ksb_p, vsb_p = prev["sb_p"]
    ksb_s, vsb_s = prev["sb_s"]
    kfx_p, vfx_p = prev["fx_p"]
    kfx_s, vfx_s = prev["fx_s"]
    hm = _head_major
    return (y_p, y_s, hm(ksb_p), hm(vsb_p), hm(kfx_p), hm(vfx_p), logf_p, pool_p,
            hm(ksb_s), hm(vsb_s), hm(kfx_s), hm(vfx_s), logf_s, pool_s)
```
